```python
import math
import jax
import jax.numpy as jnp
from jax import lax
import numpy as np

D_MODEL = 1024
BATCH = 8
SEQ = 4096
DEPTH = 1
DEC_BATCH = 128
DEC_SEQ = 1
PAST_LEN = 8192
PAGE_SIZE = 128

A_HEADS = 4
A_KV_HEADS = 4
A_HEAD_DIM = 64
B_HEADS = 8
B_KV_HEADS = 2
B_HEAD_DIM = 64
IDX_HEADS = 8
IDX_DIM = 64
TOPK_MAX = 256
N_EXPERTS = 32
TOP_K = 4
D_FF = 1024
SWIGLU_LIMIT = 7.0
SWIGLU_ALPHA = 1.702
ROPE_THETA = 10000.0
LN_EPS = 1e-5
RMS_EPS = 1e-5
Q_BLOCK = 128
DEEPNORM_ALPHA = (2.0 * DEPTH) ** 0.25
DEEPNORM_BETA = (8.0 * DEPTH) ** -0.25
IDX_SCALE = (IDX_HEADS * IDX_DIM) ** -0.5

A_Q = A_HEADS * 2 * A_HEAD_DIM
A_KV = A_KV_HEADS * 2 * A_HEAD_DIM
B_Q = B_HEADS * B_HEAD_DIM
B_KV = B_KV_HEADS * B_HEAD_DIM
IN_SPLITS = (A_Q, A_KV, A_KV, B_Q, B_KV, B_KV, IDX_HEADS * IDX_DIM, IDX_DIM, IDX_HEADS)
IN_COLS = sum(IN_SPLITS)
MIX_WIDTH = A_HEADS * 2 * A_HEAD_DIM + B_HEADS * B_HEAD_DIM

kernel_name = "hymba_diff_dsa_moe_step"


def _layer_norm(x, g, b):
    xf = x.astype(jnp.float32)
    mu = jnp.mean(xf, axis=-1, keepdims=True)
    var = jnp.mean(jnp.square(xf - mu), axis=-1, keepdims=True)
    return ((xf - mu) * lax.rsqrt(var + LN_EPS) * g.astype(jnp.float32) + b.astype(jnp.float32)).astype(x.dtype)


def _rope(x, pos):
    d = x.shape[-1]
    inv = ROPE_THETA ** (-jnp.arange(0, d, 2, dtype=jnp.float32) / d)
    ang = pos.astype(jnp.float32)[:, None] * inv[None, :]
    shape = (pos.shape[0],) + (1,) * (x.ndim - 3) + (d,)
    cos = jnp.concatenate([jnp.cos(ang), jnp.cos(ang)], axis=-1).reshape(shape)
    sin = jnp.concatenate([jnp.sin(ang), jnp.sin(ang)], axis=-1).reshape(shape)
    xf = x.astype(jnp.float32)
    rot = jnp.concatenate([-xf[..., d // 2:], xf[..., : d // 2]], axis=-1)
    return (xf * cos + rot * sin).astype(x.dtype)


def _lambda_init(layer):
    return 0.8 - 0.6 * math.exp(-0.3 * layer)


def _diff_lambda(lq1, lk1, lq2, lk2, lam_init):
    f = jnp.float32
    return (jnp.exp(jnp.sum(lq1.astype(f) * lk1.astype(f))) - jnp.exp(jnp.sum(lq2.astype(f) * lk2.astype(f))) + lam_init)


def _project(x, w_in, pos):
    b, t, _ = x.shape
    h = jnp.einsum('btd,dc->btc', x, w_in)
    offs = [int(o) for o in np.cumsum(IN_SPLITS)[:-1]]
    qa, ka, va, qb, kb, vb, qi, ki, wi = jnp.split(h, offs, axis=-1)
    qa = _rope(qa.reshape(b, t, A_HEADS, 2, A_HEAD_DIM), pos)
    ka = _rope(ka.reshape(b, t, A_KV_HEADS, 2, A_HEAD_DIM), pos)
    va = va.reshape(b, t, A_KV_HEADS, 2 * A_HEAD_DIM)
    qb = _rope(qb.reshape(b, t, B_HEADS, B_HEAD_DIM), pos)
    kb = _rope(kb.reshape(b, t, B_KV_HEADS, B_HEAD_DIM), pos)
    vb = vb.reshape(b, t, B_KV_HEADS, B_HEAD_DIM)
    qi = _rope(qi.reshape(b, t, IDX_HEADS, IDX_DIM), pos)
    ki = _rope(ki, pos)
    return (qa, ka, va, qb, kb, vb, qi, ki, wi)


def _cache_rows(proj):
    qa, ka, va, qb, kb, vb, qi, ki, wi = proj
    b, t = ka.shape[:2]
    return (ka.reshape(b, t, A_KV_HEADS, 2 * A_HEAD_DIM), va, kb, vb, ki)


def _diff_attend(qa, ka, va, mask, lam, subln_g, lam_init):
    b, tq = qa.shape[:2]
    grp = A_HEADS // A_KV_HEADS
    q = qa.reshape(b, tq, A_KV_HEADS, grp, 2, A_HEAD_DIM)
    s = jnp.einsum('btkgcd,bskcd->bkgcts', q, ka).astype(jnp.float32) * (A_HEAD_DIM ** -0.5)
    s = jnp.where(mask, s, -jnp.inf)
    p = jax.nn.softmax(s, axis=-1)
    p = p[:, :, :, 0] - lam * p[:, :, :, 1]
    o = jnp.einsum('bkgts,bskv->btkgv', p.astype(va.dtype), va)
    o = o.reshape(b, tq, A_HEADS, 2 * A_HEAD_DIM).astype(jnp.float32)
    o = o * lax.rsqrt(jnp.mean(jnp.square(o), axis=-1, keepdims=True) + RMS_EPS)
    o = o * subln_g.astype(jnp.float32) * (1.0 - lam_init)
    return o.reshape(b, tq, A_HEADS * 2 * A_HEAD_DIM).astype(va.dtype)


def _index_select(qi, wi, ki, q_pos, k_pos, topk):
    dots = jnp.einsum('bthd,bsd->bths', qi.astype(jnp.float32), ki.astype(jnp.float32))
    score = jnp.einsum('bths,bth->bts', jax.nn.relu(dots), wi.astype(jnp.float32)) * IDX_SCALE
    admissible = k_pos[None, :] <= q_pos[:, None]
    score = jnp.where(admissible[None], score, -jnp.inf)
    _, idx = lax.top_k(score, topk)
    valid = idx <= q_pos[None, :, None]
    return idx, valid


def _gather_rows(rows, idx):
    return jax.vmap(lambda r, i: r[i])(rows, idx)


def _sparse_attend(qb, k_sel, v_sel, valid):
    b, tq = qb.shape[:2]
    grp = B_HEADS // B_KV_HEADS
    q = qb.reshape(b, tq, B_KV_HEADS, grp, B_HEAD_DIM)
    s = jnp.einsum('btkgd,btnkd->btkgn', q, k_sel).astype(jnp.float32) * (B_HEAD_DIM ** -0.5)
    s = jnp.where(valid[:, :, None, None, :], s, -jnp.inf)
    p = jax.nn.softmax(s, axis=-1)
    o = jnp.einsum('btkgn,btnkd->btkgd', p.astype(v_sel.dtype), v_sel)
    return o.reshape(b, tq, B_HEADS * B_HEAD_DIM)


def _to_blocks(x):
    b, t = x.shape[:2]
    return jnp.moveaxis(x.reshape((b, t // Q_BLOCK, Q_BLOCK) + x.shape[2:]), 1, 0)


def _prompt_mixer(proj, lam, subln_g, lam_init):
    qa, ka, va, qb, kb, vb, qi, ki, wi = proj
    b, s = ka.shape[:2]
    n_blk = s // Q_BLOCK
    topk = min(TOPK_MAX, s // 4)
    k_pos = jnp.arange(s)

    def blk(args):
        i, qa_b, qb_b, qi_b, wi_b = args
        q_pos = i * Q_BLOCK + jnp.arange(Q_BLOCK)
        mask = k_pos[None, :] <= q_pos[:, None]
        oa = _diff_attend(qa_b, ka, va, mask, lam, subln_g, lam_init)
        idx, valid = _index_select(qi_b, wi_b, ki, q_pos, k_pos, topk)
        ob = _sparse_attend(qb_b, _gather_rows(kb, idx), _gather_rows(vb, idx), valid)
        return jnp.concatenate([oa, ob.astype(oa.dtype)], axis=-1)

    out = lax.map(blk, (jnp.arange(n_blk), _to_blocks(qa), _to_blocks(qb), _to_blocks(qi), _to_blocks(wi)))
    return jnp.moveaxis(out, 0, 1).reshape(b, s, MIX_WIDTH)


def _sample_mixer(layer, proj, cache_k_diff, cache_v_diff, cache_k_sparse, cache_v_sparse, cache_k_index, page_table, lam, subln_g, lam_init):
    qa, ka, va, qb, kb, vb, qi, ki, wi = proj
    n_new = qa.shape[1]
    n_keys = PAST_LEN + n_new
    topk = min(TOPK_MAX, n_keys // 4)
    q_pos = PAST_LEN + jnp.arange(n_new)
    k_pos = jnp.arange(n_keys)
    mask = k_pos[None, :] <= q_pos[:, None]

    def with_past(cache, pages, new):
        rows = cache[layer, pages]
        rows = rows.reshape((PAST_LEN,) + rows.shape[2:])
        return jnp.concatenate([rows, new.astype(rows.dtype)], axis=0)[None]

    def one(args):
        pages, qa_i, ka_i, va_i, qb_i, kb_i, vb_i, qi_i, ki_i, wi_i = args
        ka_all = with_past(cache_k_diff, pages, ka_i.reshape(n_new, A_KV_HEADS, 2 * A_HEAD_DIM))
        ka_all = ka_all.reshape(1, n_keys, A_KV_HEADS, 2, A_HEAD_DIM)
        va_all = with_past(cache_v_diff, pages, va_i)
        kb_all = with_past(cache_k_sparse, pages, kb_i)
        vb_all = with_past(cache_v_sparse, pages, vb_i)
        ki_all = with_past(cache_k_index, pages, ki_i)
        oa = _diff_attend(qa_i[None], ka_all, va_all, mask, lam, subln_g, lam_init)
        idx, valid = _index_select(qi_i[None], wi_i[None], ki_all, q_pos, k_pos, topk)
        ob = _sparse_attend(qb_i[None], _gather_rows(kb_all, idx), _gather_rows(vb_all, idx), valid)
        return jnp.concatenate([oa, ob.astype(oa.dtype)], axis=-1)[0]

    return lax.map(one, (page_table, qa, ka, va, qb, kb, vb, qi, ki, wi))


def _moe(x, router_w, router_b, w_gate, b_gate, w_up, b_up, w_down, b_down):
    b, t, d = x.shape
    xt = x.reshape(b * t, d)
    logits = (xt @ router_w + router_b).astype(jnp.float32)
    vals, idx = lax.top_k(logits, TOP_K)
    gates = jax.nn.softmax(vals, axis=-1)
    combine = jnp.sum(jax.nn.one_hot(idx, N_EXPERTS, dtype=jnp.float32) * gates[..., None], axis=1).astype(x.dtype)

    def step(acc, e):
        wg, bg, wu, bu, wd, bd, c = e
        g = jnp.minimum(xt @ wg + bg, SWIGLU_LIMIT)
        u = jnp.clip(xt @ wu + bu, -SWIGLU_LIMIT, SWIGLU_LIMIT)
        h = g * jax.nn.sigmoid(SWIGLU_ALPHA * g) * (u + 1.0)
        return acc + c[:, None] * (h @ wd + bd), None

    y, _ = lax.scan(step, jnp.zeros_like(xt), (w_gate, b_gate, w_up, b_up, w_down, b_down, combine.T))
    return y.reshape(b, t, d)


def _post_layer(x, mix, w_out, ln1_g, ln1_b, router_w, router_b, w_gate, b_gate, w_up, b_up, w_down, b_down, ln2_g, ln2_b):
    h = _layer_norm(DEEPNORM_ALPHA * x + jnp.einsum('btc,cd->btd', mix, w_out), ln1_g, ln1_b)
    return _layer_norm(DEEPNORM_ALPHA * h + _moe(h, router_w, router_b, w_gate, b_gate, w_up, b_up, w_down, b_down), ln2_g, ln2_b)


def setup_inputs(seed: int = 0) -> dict:
    key = jax.random.key(seed)
    ks = jax.random.split(key, 32)
    f = jnp.float32
    n_pages = PAST_LEN // PAGE_SIZE
    n_used = DEC_BATCH * n_pages
    n_pool = n_used + max(1, n_used // 4)
    nrm = lambda k, s, sc: jax.random.normal(k, s, f) * sc
    page_table = jax.random.permutation(ks[0], n_pool)[:n_used].reshape(DEC_BATCH, n_pages).astype(jnp.int32)
    return {
        "x_prompt": nrm(ks[1], (BATCH, SEQ, D_MODEL), 1.0),
        "x_sample": nrm(ks[2], (DEC_BATCH, DEC_SEQ, D_MODEL), 1.0),
        "cache_k_diff": nrm(ks[3], (DEPTH, n_pool, PAGE_SIZE, A_KV_HEADS, 2 * A_HEAD_DIM), 1.0),
        "cache_v_diff": nrm(ks[4], (DEPTH, n_pool, PAGE_SIZE, A_KV_HEADS, 2 * A_HEAD_DIM), 1.0),
        "cache_k_sparse": nrm(ks[5], (DEPTH, n_pool, PAGE_SIZE, B_KV_HEADS, B_HEAD_DIM), 1.0),
        "cache_v_sparse": nrm(ks[6], (DEPTH, n_pool, PAGE_SIZE, B_KV_HEADS, B_HEAD_DIM), 1.0),
        "cache_k_index": nrm(ks[7], (DEPTH, n_pool, PAGE_SIZE, IDX_DIM), 1.0),
        "page_table": page_table,
        "w_in": nrm(ks[8], (DEPTH, D_MODEL, IN_COLS), D_MODEL ** -0.5),
        "lambda_q1": nrm(ks[9], (DEPTH, A_HEAD_DIM), 0.1),
        "lambda_k1": nrm(ks[10], (DEPTH, A_HEAD_DIM), 0.1),
        "lambda_q2": nrm(ks[11], (DEPTH, A_HEAD_DIM), 0.1),
        "lambda_k2": nrm(ks[12], (DEPTH, A_HEAD_DIM), 0.1),
        "subln_g": 1.0 + nrm(ks[13], (DEPTH, 2 * A_HEAD_DIM), 0.02),
        "w_out": nrm(ks[14], (DEPTH, MIX_WIDTH, D_MODEL), DEEPNORM_BETA * MIX_WIDTH ** -0.5),
        "ln1_g": 1.0 + nrm(ks[15], (DEPTH, D_MODEL), 0.02),
        "ln1_b": nrm(ks[16], (DEPTH, D_MODEL), 0.02),
        "router_w": nrm(ks[17], (DEPTH, D_MODEL, N_EXPERTS), D_MODEL ** -0.5),
        "router_b": nrm(ks[18], (DEPTH, N_EXPERTS), 0.01),
        "w_gate": nrm(ks[19], (DEPTH, N_EXPERTS, D_MODEL, D_FF), D_MODEL ** -0.5),
        "b_gate": nrm(ks[20], (DEPTH, N_EXPERTS, D_FF), 0.02),
        "w_up": nrm(ks[21], (DEPTH, N_EXPERTS, D_MODEL, D_FF), D_MODEL ** -0.5),
        "b_up": nrm(ks[22], (DEPTH, N_EXPERTS, D_FF), 0.02),
        "w_down": nrm(ks[23], (DEPTH, N_EXPERTS, D_FF, D_MODEL), DEEPNORM_BETA * D_FF ** -0.5),
        "b_down": nrm(ks[24], (DEPTH, N_EXPERTS, D_MODEL), 0.02),
        "ln2_g": 1.0 + nrm(ks[25], (DEPTH, D_MODEL), 0.02),
        "ln2_b": nrm(ks[26], (DEPTH, D_MODEL), 0.02),
    }


def reference(x_prompt, x_sample, cache_k_diff, cache_v_diff, cache_k_sparse, cache_v_sparse, cache_k_index, page_table, w_in, lambda_q1, lambda_k1, lambda_q2, lambda_k2, subln_g, w_out, ln1_g, ln1_b, router_w, router_b, w_gate, b_gate, w_up, b_up, w_down, b_down, ln2_g, ln2_b):
    pos_p = jnp.arange(x_prompt.shape[1])
    pos_s = PAST_LEN + jnp.arange(x_sample.shape[1])
    xp, xs = x_prompt, x_sample
    rows_p, rows_s = [], []
    for l in range(DEPTH):
        lam_init = _lambda_init(l)
        lam = _diff_lambda(lambda_q1[l], lambda_k1[l], lambda_q2[l], lambda_k2[l], lam_init)
        proj_p = _project(xp, w_in[l], pos_p)
        proj_s = _project(xs, w_in[l], pos_s)
        mix_p = _prompt_mixer(proj_p, lam, subln_g[l], lam_init)
        mix_s = _sample_mixer(l, proj_s, cache_k_diff, cache_v_diff, cache_k_sparse, cache_v_sparse, cache_k_index, page_table, lam, subln_g[l], lam_init)
        rows_p.append(_cache_rows(proj_p))
        rows_s.append(_cache_rows(proj_s))
        xp = _post_layer(xp, mix_p, w_out[l], ln1_g[l], ln1_b[l], router_w[l], router_b[l], w_gate[l], b_gate[l], w_up[l], b_up[l], w_down[l], b_down[l], ln2_g[l], ln2_b[l])
        xs = _post_layer(xs, mix_s, w_out[l], ln1_g[l], ln1_b[l], router_w[l], router_b[l], w_gate[l], b_gate[l], w_up[l], b_up[l], w_down[l], b_down[l], ln2_g[l], ln2_b[l])
    k_diff_p = jnp.stack([r[0] for r in rows_p], axis=0)
    v_diff_p = jnp.stack([r[1] for r in rows_p], axis=0)
    k_sparse_p = jnp.stack([r[2] for r in rows_p], axis=0)
    v_sparse_p = jnp.stack([r[3] for r in rows_p], axis=0)
    k_index_p = jnp.stack([r[4] for r in rows_p], axis=0)
    k_diff_s = jnp.stack([r[0] for r in rows_s], axis=0)
    v_diff_s = jnp.stack([r[1] for r in rows_s], axis=0)
    k_sparse_s = jnp.stack([r[2] for r in rows_s], axis=0)
    v_sparse_s = jnp.stack([r[3] for r in rows_s], axis=0)
    k_index_s = jnp.stack([r[4] for r in rows_s], axis=0)
    return (xp, xs, k_diff_p, v_diff_p, k_sparse_p, v_sparse_p, k_index_p, k_diff_s, v_diff_s, k_sparse_s, v_sparse_s, k_index_s)
```

```python
import functools
import math

import jax
import jax.numpy as jnp
from jax import lax
from jax.experimental import pallas as pl
from jax.experimental.pallas import tpu as pltpu

F32 = jnp.float32
BF16 = jnp.bfloat16
I32 = jnp.int32

A_HEADS = 4
A_HEAD_DIM = 64
B_HEADS = 8
B_KV_HEADS = 2
B_HEAD_DIM = 64
IDX_HEADS = 8
IDX_DIM = 64
TOPK_MAX = 256
N_EXPERTS = 32
TOP_K = 4
SWIGLU_LIMIT = 7.0
SWIGLU_ALPHA = 1.702
ROPE_THETA = 10000.0
LN_EPS = 1e-5
RMS_EPS = 1e-5
DEPTH = 1
DEEPNORM_ALPHA = (2.0 * DEPTH) ** 0.25
IDX_SCALE = (IDX_HEADS * IDX_DIM) ** -0.5

LANES = 128
HALF = 64
NEG_BIG = -1e30
INT_MIN = -(2 ** 31)
VMEM_LIMIT = 56 * 1024 * 1024

_C_QA, _C_KA, _C_VA, _C_QB, _C_KB, _C_VB, _C_QI, _C_KIWI = 0, 512, 1024, 1536, 2048, 2176, 2304, 2816
_IN_COLS = 2888
_IN_COLS_PAD = 2944


def _cparams(sem):
    return pltpu.CompilerParams(dimension_semantics=sem, vmem_limit_bytes=VMEM_LIMIT)


def _nt_dot(a, b):
    return lax.dot_general(a, b, (((1,), (1,)), ((), ())), preferred_element_type=F32)


def _tn_dot(a, b):
    return lax.dot_general(a, b, (((0,), (0,)), ((), ())), preferred_element_type=F32)


def _lane_iota(shape):
    return lax.broadcasted_iota(I32, shape, 1)


def _proj_kernel(x_ref, w_ref, cos_ref, sin_ref,
                 qa_ref, kaf_ref, kab_ref, vaf_ref, vab_ref, qb_ref, kbf_ref, kbb_ref, vbf_ref, vbb_ref,
                 qi_ref, kif_ref, kib_ref, wi_ref):
    xb = x_ref[...].astype(BF16)
    cos = cos_ref[...]
    sin = sin_ref[...]
    lane = _lane_iota(cos.shape)
    first = (lane % HALF) < (HALF // 2)

    def seg(c0, n):
        return jnp.dot(xb, w_ref[:, c0:c0 + n], preferred_element_type=F32)

    def rope(h):
        rot = jnp.where(first, pltpu.roll(h, LANES - HALF // 2, 1), pltpu.roll(h, HALF // 2, 1))
        return h * cos + rot * sin

    def blocks(h):
        return [h[:, k * LANES:(k + 1) * LANES] for k in range(h.shape[1] // LANES)]

    scale = A_HEAD_DIM ** -0.5
    for k, h in enumerate(blocks(seg(_C_QA, 512))):
        qa_ref[:, k * LANES:(k + 1) * LANES] = (rope(h) * scale).astype(BF16)
    for k, h in enumerate(blocks(seg(_C_KA, 512))):
        r = rope(h)
        kaf_ref[:, k * LANES:(k + 1) * LANES] = r
        kab_ref[:, k * LANES:(k + 1) * LANES] = r.astype(BF16)
    va = seg(_C_VA, 512)
    vaf_ref[...] = va
    vab_ref[...] = va.astype(BF16)
    for k, h in enumerate(blocks(seg(_C_QB, 512))):
        qb_ref[:, k * LANES:(k + 1) * LANES] = (rope(h) * scale).astype(BF16)
    kbvb = seg(_C_KB, 256)
    kb = rope(kbvb[:, :LANES])
    kbf_ref[...] = kb
    kbb_ref[...] = kb.astype(BF16)
    vb = kbvb[:, LANES:]
    vbf_ref[...] = vb
    vbb_ref[...] = vb.astype(BF16)
    for k, h in enumerate(blocks(seg(_C_QI, 512))):
        qi_ref[:, k * LANES:(k + 1) * LANES] = rope(h).astype(BF16)
    kiwi = seg(_C_KIWI, LANES)
    ki = rope(kiwi)
    kif_ref[...] = ki[:, :IDX_DIM]
    kib_ref[...] = jnp.where(lane < HALF, ki, pltpu.roll(ki, HALF, 1)).astype(BF16)
    wi_ref[...] = pltpu.roll(kiwi, HALF, 1)[:, :IDX_HEADS]


def _project(x, w_pad, cos, sin, tm):
    n, d = x.shape
    t_tab = cos.shape[0]
    n_tab = t_tab // tm
    row = lambda i: (i, 0)
    shapes = [((n, 512), BF16), ((n, 512), F32), ((n, 512), BF16), ((n, 512), F32), ((n, 512), BF16),
              ((n, 512), BF16), ((n, LANES), F32), ((n, LANES), BF16), ((n, LANES), F32), ((n, LANES), BF16),
              ((n, 512), BF16), ((n, IDX_DIM), F32), ((n, LANES), BF16), ((n, IDX_HEADS), F32)]
    return pl.pallas_call(
        _proj_kernel,
        grid=(n // tm,),
        in_specs=[pl.BlockSpec((tm, d), row),
                  pl.BlockSpec((d, _IN_COLS_PAD), lambda i: (0, 0)),
                  pl.BlockSpec((tm, LANES), lambda i: (i % n_tab, 0)),
                  pl.BlockSpec((tm, LANES), lambda i: (i % n_tab, 0))],
        out_specs=[pl.BlockSpec((tm, s[1]), row) for s, _ in shapes],
        out_shape=[jax.ShapeDtypeStruct(s, dt) for s, dt in shapes],
        compiler_params=_cparams(("parallel",)),
        name="proj",
    )(x, w_pad, cos, sin)


def _rope_tables(pos):
    inv = ROPE_THETA ** (-jnp.arange(0, HALF, 2, dtype=F32) / HALF)
    ang = pos.astype(F32)[:, None] * inv[None, :]
    cos = jnp.concatenate([jnp.cos(ang)] * 4, axis=-1)
    sin = jnp.concatenate([-jnp.sin(ang), jnp.sin(ang)] * 2, axis=-1)
    return cos, sin


def _diff_finish(acc, l, lam, g, lam_init, rows):
    o = acc[:rows] / l[:rows] - lam * (acc[rows:] / l[rows:])
    o = o * lax.rsqrt(jnp.mean(o * o, axis=-1, keepdims=True) + RMS_EPS)
    return o * g * (1.0 - lam_init)


def _diff_kernel(lam_ref, q_ref, k_ref, v_ref, g_ref, o_ref, m_scr, l_scr, acc_scr, *, tq, lam_init):
    qi = pl.program_id(2)
    q = q_ref[0]
    lane = _lane_iota(q.shape)
    zero = jnp.zeros_like(q)
    qq = jnp.concatenate([jnp.where(lane < HALF, q, zero), jnp.where(lane >= HALF, q, zero)], axis=0)
    m_scr[...] = jnp.full(m_scr.shape, NEG_BIG, F32)
    l_scr[...] = jnp.zeros(l_scr.shape, F32)
    acc_scr[...] = jnp.zeros(acc_scr.shape, F32)

    def step(j, diagonal):
        off = pl.multiple_of(j * tq, tq)
        k = k_ref[0, pl.ds(off, tq), :]
        v = v_ref[0, pl.ds(off, tq), :]
        s = _nt_dot(qq, k)
        if diagonal:
            r = lax.broadcasted_iota(I32, s.shape, 0) % tq
            c = lax.broadcasted_iota(I32, s.shape, 1)
            s = jnp.where(c <= r, s, NEG_BIG)
        m_old = m_scr[...]
        m_new = jnp.maximum(m_old, jnp.max(s, axis=1, keepdims=True))
        a = jnp.exp(m_old - m_new)
        p = jnp.exp(s - m_new)
        l_scr[...] = a * l_scr[...] + jnp.sum(p, axis=1, keepdims=True)
        acc_scr[...] = a * acc_scr[...] + jnp.dot(p.astype(BF16), v, preferred_element_type=F32)
        m_scr[...] = m_new

    def body(j, carry):
        step(j, False)
        return carry

    lax.fori_loop(0, qi, body, 0)
    step(qi, True)
    o = _diff_finish(acc_scr[...], l_scr[...], lam_ref[0, 0], g_ref[...], lam_init, tq)
    o_ref[0] = o.astype(o_ref.dtype)


def _diff_attention(lam, qa, ka, va, g, lam_init, tq):
    b, t, _ = qa.shape
    return pl.pallas_call(
        functools.partial(_diff_kernel, tq=tq, lam_init=lam_init),
        grid=(b, A_HEADS, t // tq),
        in_specs=[pl.BlockSpec(memory_space=pltpu.SMEM),
                  pl.BlockSpec((1, tq, LANES), lambda b_, h, i: (b_, i, h)),
                  pl.BlockSpec((1, t, LANES), lambda b_, h, i: (b_, 0, h)),
                  pl.BlockSpec((1, t, LANES), lambda b_, h, i: (b_, 0, h)),
                  pl.BlockSpec((1, LANES), lambda b_, h, i: (0, 0))],
        out_specs=pl.BlockSpec((1, tq, LANES), lambda b_, h, i: (b_, i, h)),
        out_shape=jax.ShapeDtypeStruct((b, t, A_HEADS * LANES), BF16),
        scratch_shapes=[pltpu.VMEM((2 * tq, 1), F32), pltpu.VMEM((2 * tq, 1), F32),
                        pltpu.VMEM((2 * tq, LANES), F32)],
        compiler_params=_cparams(("parallel", "parallel", "arbitrary")),
        name="diff_attn",
    )(lam, qa, ka, va, g)


def _float_key(x):
    b = lax.bitcast_convert_type(x + 0.0, I32)
    return b ^ ((b >> 31) & 0x7FFFFFFF)


def _select_topk(key_scr, nch, ck, topk):
    rows = key_scr.shape[0]
    nl = ck // LANES
    kf = float(topk)

    def counts(tau_b, strict):
        def body(c, cnt):
            off = pl.multiple_of(c * ck, ck)
            for g in range(nl):
                kk = key_scr[:, pl.ds(off + g * LANES, LANES)]
                hit = (kk > tau_b) if strict else (kk >= tau_b)
                cnt = cnt + jnp.where(hit, 1, 0)
            return cnt
        cnt = lax.fori_loop(0, nch, body, jnp.zeros((rows, LANES), I32))
        return jnp.sum(cnt.astype(F32), axis=1, keepdims=True)

    def bcast(v):
        return jnp.broadcast_to(v, (rows, LANES))

    cnt0 = counts(jnp.zeros((rows, LANES), I32), False)
    tau = jnp.where(cnt0 >= kf, 0, INT_MIN).astype(I32)

    def bit_body(i, tau):
        cand = tau | jnp.left_shift(jnp.int32(1), 30 - i)
        cnt = counts(bcast(cand), False)
        return jnp.where(cnt >= kf, cand, tau)

    tau = lax.fori_loop(0, 31, bit_body, tau)
    tau_b = bcast(tau)
    cnt_ge = counts(tau_b, False)
    cnt_gt = counts(tau_b, True)
    need_b = bcast(kf - cnt_gt)
    overfull = jnp.where((cnt_ge > kf) & (tau > INT_MIN), 1.0, 0.0)

    @pl.when(jnp.max(overfull) > 0.0)
    def _():
        upper = (lax.broadcasted_iota(I32, (LANES, LANES), 0)
                 <= lax.broadcasted_iota(I32, (LANES, LANES), 1)).astype(BF16)

        def body(c, run):
            off = pl.multiple_of(c * ck, ck)
            for g in range(nl):
                sl = pl.ds(off + g * LANES, LANES)
                kk = key_scr[:, sl]
                eq = kk == tau_b
                eqf = jnp.where(eq, 1.0, 0.0)
                pre = jnp.dot(eqf.astype(BF16), upper, preferred_element_type=F32)
                drop = eq & ((pre + run) > need_b)
                key_scr[:, sl] = jnp.where(drop, INT_MIN, kk)
                run = run + jnp.sum(eqf, axis=1, keepdims=True)
            return run

        lax.fori_loop(0, nch, body, jnp.zeros((rows, 1), F32))

    return tau


def _stack_heads(q, n_heads):
    lane = _lane_iota((q.shape[0], LANES))
    zero = jnp.zeros((q.shape[0], LANES), q.dtype)
    parts = []
    for h in range(n_heads):
        blk = q[:, (h // 2) * LANES:(h // 2 + 1) * LANES]
        parts.append(jnp.where((lane < HALF) == (h % 2 == 0), blk, zero))
    return jnp.concatenate(parts, axis=0)


def _sparse_kernel(qi_ref, wi_ref, ki_ref, qb_ref, kb_ref, vb_ref, o_ref, key_scr, m_scr, l_scr, acc_scr,
                   *, tq, ck, topk):
    i = pl.program_id(1)
    nch = ((i + 1) * tq + ck - 1) // ck
    row = i * tq + lax.broadcasted_iota(I32, (tq, ck), 0)
    col0 = lax.broadcasted_iota(I32, (tq, ck), 1)

    qs = _stack_heads(qi_ref[0], IDX_HEADS)
    w = wi_ref[0]

    def score_body(c, carry):
        off = pl.multiple_of(c * ck, ck)
        d = _nt_dot(qs, ki_ref[0, pl.ds(off, ck), :])
        sc = jnp.zeros((tq, ck), F32)
        for h in range(IDX_HEADS):
            sc = sc + jnp.maximum(d[h * tq:(h + 1) * tq], 0.0) * w[:, h:h + 1]
        sc = sc * IDX_SCALE
        key_scr[:, pl.ds(off, ck)] = jnp.where(col0 + off <= row, _float_key(sc), INT_MIN)
        return carry

    lax.fori_loop(0, nch, score_body, 0)
    tau = _select_topk(key_scr, nch, ck, topk)
    tau_b = jnp.broadcast_to(tau, (tq, ck))

    qh = []
    qb = qb_ref[0]
    lane = _lane_iota((tq, LANES))
    zero = jnp.zeros((tq, LANES), qb.dtype)
    grp = B_HEADS // B_KV_HEADS
    for h in range(B_HEADS):
        blk = qb[:, (h // 2) * LANES:(h // 2 + 1) * LANES]
        if h % 2 != h // grp:
            blk = pltpu.roll(blk, HALF, 1)
        qh.append(jnp.where((lane >= HALF) == (h // grp == 1), blk, zero))
    qs2 = jnp.concatenate(qh, axis=0)
    m_scr[...] = jnp.full(m_scr.shape, NEG_BIG, F32)
    l_scr[...] = jnp.zeros(l_scr.shape, F32)
    acc_scr[...] = jnp.zeros(acc_scr.shape, F32)

    def attn_body(c, carry):
        off = pl.multiple_of(c * ck, ck)
        sel = (key_scr[:, pl.ds(off, ck)] >= tau_b) & (col0 + off <= row)
        s = _nt_dot(qs2, kb_ref[0, pl.ds(off, ck), :])
        v = vb_ref[0, pl.ds(off, ck), :]
        for h in range(B_HEADS):
            sh = jnp.where(sel, s[h * tq:(h + 1) * tq], NEG_BIG)
            m_old = m_scr[h]
            m_new = jnp.maximum(m_old, jnp.max(sh, axis=1, keepdims=True))
            a = jnp.exp(m_old - m_new)
            p = jnp.where(sel, jnp.exp(sh - m_new), 0.0)
            l_scr[h] = a * l_scr[h] + jnp.sum(p, axis=1, keepdims=True)
            acc_scr[h] = a * acc_scr[h] + jnp.dot(p.astype(BF16), v, preferred_element_type=F32)
            m_scr[h] = m_new
        return carry

    lax.fori_loop(0, nch, attn_body, 0)
    for j in range(B_HEADS // 2):
        ha, hb = 2 * j, 2 * j + 1
        oa = acc_scr[ha] / l_scr[ha]
        ob = acc_scr[hb] / l_scr[hb]
        if ha // grp == 0:
            o = jnp.where(lane < HALF, oa, pltpu.roll(ob, HALF, 1))
        else:
            o = jnp.where(lane < HALF, pltpu.roll(oa, HALF, 1), ob)
        o_ref[0, :, j * LANES:(j + 1) * LANES] = o.astype(o_ref.dtype)


def _sparse_attention(qi, wi, ki2, qb, kb, vb, tq, ck):
    b, t, _ = qi.shape
    topk = min(TOPK_MAX, t // 4)
    blk_q = lambda w: pl.BlockSpec((1, tq, w), lambda b_, i: (b_, i, 0))
    blk_k = pl.BlockSpec((1, t, LANES), lambda b_, i: (b_, 0, 0))
    return pl.pallas_call(
        functools.partial(_sparse_kernel, tq=tq, ck=ck, topk=topk),
        grid=(b, t // tq),
        in_specs=[blk_q(512), blk_q(IDX_HEADS), blk_k, blk_q(512), blk_k, blk_k],
        out_specs=blk_q(512),
        out_shape=jax.ShapeDtypeStruct((b, t, 512), BF16),
        scratch_shapes=[pltpu.VMEM((tq, t), I32), pltpu.VMEM((B_HEADS, tq, 1), F32),
                        pltpu.VMEM((B_HEADS, tq, 1), F32), pltpu.VMEM((B_HEADS, tq, LANES), F32)],
        compiler_params=_cparams(("parallel", "arbitrary")),
        name="sparse_attn",
    )(qi, wi, ki2, qb, kb, vb)


def _page_specs(n, block, pg):
    zeros = (0,) * (len(block) - 1)
    return [pl.BlockSpec(block, functools.partial(lambda i, b, j, pt: (pt[b, j * pg + i],) + zeros, i))
            for i in range(n)]


def _dec_diff_kernel(pt_ref, lam_ref, q_ref, kn_ref, vn_ref, g_ref, *rest, pg, lam_init):
    k_refs, v_refs = rest[:pg], rest[pg:2 * pg]
    o_ref, m_scr, l_scr, acc_scr = rest[2 * pg:]
    j = pl.program_id(1)

    @pl.when(j == 0)
    def _():
        m_scr[...] = jnp.full(m_scr.shape, NEG_BIG, F32)
        l_scr[...] = jnp.zeros(l_scr.shape, F32)
        acc_scr[...] = jnp.zeros(acc_scr.shape, F32)

    q = q_ref[0]
    rows_pg = k_refs[0].shape[1]
    s = jnp.concatenate([_nt_dot(q, k[0].astype(BF16)) for k in k_refs], axis=1)
    same_head = (_lane_iota(s.shape) % A_HEADS) == (lax.broadcasted_iota(I32, s.shape, 0) % A_HEADS)
    s = jnp.where(same_head, s, NEG_BIG)
    m_old = m_scr[...]
    m_new = jnp.maximum(m_old, jnp.max(s, axis=1, keepdims=True))
    a = jnp.exp(m_old - m_new)
    p = jnp.exp(s - m_new).astype(BF16)
    pv = jnp.zeros(acc_scr.shape, F32)
    for i, v in enumerate(v_refs):
        pv = pv + jnp.dot(p[:, i * rows_pg:(i + 1) * rows_pg], v[0].astype(BF16), preferred_element_type=F32)
    l_scr[...] = a * l_scr[...] + jnp.sum(p.astype(F32), axis=1, keepdims=True)
    acc_scr[...] = a * acc_scr[...] + pv
    m_scr[...] = m_new

    @pl.when(j == pl.num_programs(1) - 1)
    def _():
        s_new = jnp.sum(q.astype(F32) * kn_ref[0], axis=1, keepdims=True)
        m_old = m_scr[...]
        m_new = jnp.maximum(m_old, s_new)
        a = jnp.exp(m_old - m_new)
        pn = jnp.exp(s_new - m_new)
        l = a * l_scr[...] + pn
        acc = a * acc_scr[...] + pn * vn_ref[0]
        o = _diff_finish(acc, l, lam_ref[0, 0], g_ref[...], lam_init, A_HEADS)
        o_ref[0] = jnp.concatenate([o, jnp.zeros_like(o)], axis=0)


def _dec_diff_attention(pt, lam, qm, knew, vnew, g, kpages, vpages, lam_init, pg):
    db, npg = pt.shape
    blk = (1,) + kpages.shape[1:]
    per_b = lambda shape: pl.BlockSpec(shape, lambda b, j, pt_: (b,) + (0,) * (len(shape) - 1))
    return pl.pallas_call(
        functools.partial(_dec_diff_kernel, pg=pg, lam_init=lam_init),
        grid_spec=pltpu.PrefetchScalarGridSpec(
            num_scalar_prefetch=1,
            grid=(db, npg // pg),
            in_specs=[pl.BlockSpec(memory_space=pltpu.SMEM), per_b((1, 8, LANES)), per_b((1, 8, LANES)),
                      per_b((1, 8, LANES)), pl.BlockSpec((1, LANES), lambda b, j, pt_: (0, 0))]
                     + _page_specs(pg, blk, pg) + _page_specs(pg, blk, pg),
            out_specs=per_b((1, 8, LANES)),
            scratch_shapes=[pltpu.VMEM((8, 1), F32), pltpu.VMEM((8, 1), F32), pltpu.VMEM((8, LANES), F32)],
        ),
        out_shape=jax.ShapeDtypeStruct((db, 8, LANES), F32),
        compiler_params=_cparams(("parallel", "arbitrary")),
        name="dec_diff_attn",
    )(pt, lam, qm, knew, vnew, g, *([kpages] * pg), *([vpages] * pg))


def _dec_index_kernel(pt_ref, q_ref, w_ref, kn_ref, *rest, pg):
    k_refs = rest[:pg]
    sc_ref, new_ref = rest[pg:]
    q = q_ref[0]
    w = w_ref[0]
    page = k_refs[0].shape[2]
    for i, k in enumerate(k_refs):
        d = jnp.dot(q, k[0].astype(BF16), preferred_element_type=F32)
        sc = jnp.sum(jnp.maximum(d, 0.0) * w, axis=0, keepdims=True) * IDX_SCALE
        sc_ref[0, :, i * page:(i + 1) * page] = sc

    @pl.when(pl.program_id(1) == pl.num_programs(1) - 1)
    def _():
        d = jnp.sum(q.astype(F32) * kn_ref[0], axis=1, keepdims=True)
        sc = jnp.sum(jnp.maximum(d, 0.0) * w, axis=0, keepdims=True) * IDX_SCALE
        new_ref[0] = jnp.broadcast_to(sc, new_ref.shape[1:])


def _dec_index_scores(pt, qi, wi, kinew, kpages, pg):
    db, npg = pt.shape
    page = kpages.shape[2]
    per_b = lambda shape: pl.BlockSpec(shape, lambda b, j, pt_: (b,) + (0,) * (len(shape) - 1))
    return pl.pallas_call(
        functools.partial(_dec_index_kernel, pg=pg),
        grid_spec=pltpu.PrefetchScalarGridSpec(
            num_scalar_prefetch=1,
            grid=(db, npg // pg),
            in_specs=[per_b((1, IDX_HEADS, IDX_DIM)), per_b((1, IDX_HEADS, 1)), per_b((1, 1, IDX_DIM))]
                     + _page_specs(pg, (1,) + kpages.shape[1:], pg),
            out_specs=[pl.BlockSpec((1, 1, pg * page), lambda b, j, pt_: (b, 0, j)), per_b((1, 1, LANES))],
        ),
        out_shape=[jax.ShapeDtypeStruct((db, 1, npg * page), F32), jax.ShapeDtypeStruct((db, 1, LANES), F32)],
        compiler_params=_cparams(("parallel", "arbitrary")),
        name="dec_index_scores",
    )(pt, qi, wi, kinew, *([kpages] * pg))


def _dec_select_kernel(sc_ref, key_ref, tau_ref, *, n_keys, topk):
    rows, width = sc_ref.shape
    col = _lane_iota((rows, width))
    key_ref[...] = jnp.where(col < n_keys, _float_key(sc_ref[...]), INT_MIN)
    tau = _select_topk(key_ref, width // LANES, LANES, topk)
    tau_ref[...] = jnp.broadcast_to(tau, tau_ref.shape)


def _dec_select(scores, n_keys, topk):
    rows, width = scores.shape
    return pl.pallas_call(
        functools.partial(_dec_select_kernel, n_keys=n_keys, topk=topk),
        out_shape=[jax.ShapeDtypeStruct((rows, width), I32), jax.ShapeDtypeStruct((rows, LANES), I32)],
        compiler_params=pltpu.CompilerParams(vmem_limit_bytes=VMEM_LIMIT),
        name="dec_select",
    )(scores)


def _dec_sparse_kernel(pt_ref, q_ref, key_ref, keyn_ref, tau_ref, kn_ref, vn_ref, *rest, pg):
    k_refs, v_refs = rest[:pg], rest[pg:2 * pg]
    o_ref, m_scr, l_scr, acc_scr = rest[2 * pg:]
    j = pl.program_id(1)

    @pl.when(j == 0)
    def _():
        m_scr[...] = jnp.full(m_scr.shape, NEG_BIG, F32)
        l_scr[...] = jnp.zeros(l_scr.shape, F32)
        acc_scr[...] = jnp.zeros(acc_scr.shape, F32)

    page = k_refs[0].shape[3]
    tau = tau_ref[0][:, 0:1]
    sel = jnp.broadcast_to(key_ref[0] >= tau, (8, pg * page))
    for g in range(B_KV_HEADS):
        q = q_ref[0, g]
        s = jnp.concatenate([jnp.dot(q, k[0, g].astype(BF16), preferred_element_type=F32) for k in k_refs], axis=1)
        s = jnp.where(sel, s, NEG_BIG)
        m_old = m_scr[g]
        m_new = jnp.maximum(m_old, jnp.max(s, axis=1, keepdims=True))
        a = jnp.exp(m_old - m_new)
        p = jnp.where(sel, jnp.exp(s - m_new), 0.0).astype(BF16)
        pv = jnp.zeros(acc_scr.shape[1:], F32)
        for i, v in enumerate(v_refs):
            pv = pv + _nt_dot(p[:, i * page:(i + 1) * page], v[0, g].astype(BF16))
        l_scr[g] = a * l_scr[g] + jnp.sum(p.astype(F32), axis=1, keepdims=True)
        acc_scr[g] = a * acc_scr[g] + pv
        m_scr[g] = m_new

    @pl.when(j == pl.num_programs(1) - 1)
    def _():
        sel_new = keyn_ref[0][:, 0:1] >= tau
        for g in range(B_KV_HEADS):
            s_new = jnp.sum(q_ref[0, g].astype(F32) * kn_ref[0, g], axis=1, keepdims=True)
            s_new = jnp.where(sel_new, s_new, NEG_BIG)
            m_old = m_scr[g]
            m_new = jnp.maximum(m_old, s_new)
            a = jnp.exp(m_old - m_new)
            pn = jnp.where(sel_new, jnp.exp(s_new - m_new), 0.0)
            l = a * l_scr[g] + pn
            o_ref[0, g] = (a * acc_scr[g] + pn * vn_ref[0, g]) / l


def _dec_sparse_attention(pt, q, keys, tau, kbnew, vbnew, kpages, vpages, pg):
    db, npg = pt.shape
    page = kpages.shape[3]
    per_b = lambda shape: pl.BlockSpec(shape, lambda b, j, pt_: (b,) + (0,) * (len(shape) - 1))
    blk = (1,) + kpages.shape[1:]
    return pl.pallas_call(
        functools.partial(_dec_sparse_kernel, pg=pg),
        grid_spec=pltpu.PrefetchScalarGridSpec(
            num_scalar_prefetch=1,
            grid=(db, npg // pg),
            in_specs=[per_b((1, B_KV_HEADS, 8, B_HEAD_DIM)),
                      pl.BlockSpec((1, 1, pg * page), lambda b, j, pt_: (b, 0, j)),
                      pl.BlockSpec((1, 1, LANES), lambda b, j, pt_: (b, 0, npg * page // LANES)),
                      per_b((1, 1, LANES)), per_b((1, B_KV_HEADS, 1, B_HEAD_DIM)),
                      per_b((1, B_KV_HEADS, 1, B_HEAD_DIM))]
                     + _page_specs(pg, blk, pg) + _page_specs(pg, blk, pg),
            out_specs=per_b((1, B_KV_HEADS, 8, B_HEAD_DIM)),
            scratch_shapes=[pltpu.VMEM((B_KV_HEADS, 8, 1), F32), pltpu.VMEM((B_KV_HEADS, 8, 1), F32),
                            pltpu.VMEM((B_KV_HEADS, 8, B_HEAD_DIM), F32)],
        ),
        out_shape=jax.ShapeDtypeStruct((db, B_KV_HEADS, 8, B_HEAD_DIM), F32),
        compiler_params=_cparams(("parallel", "arbitrary")),
        name="dec_sparse_attn",
    )(pt, q, keys, keys, tau, kbnew, vbnew, *([kpages] * pg), *([vpages] * pg))


def _layer_norm(x, g, b):
    mu = jnp.mean(x, axis=-1, keepdims=True)
    xc = x - mu
    var = jnp.mean(xc * xc, axis=-1, keepdims=True)
    return xc * lax.rsqrt(var + LN_EPS) * g + b


def _post_kernel(x_ref, ma_ref, mb_ref, woa_ref, wob_ref, g_ref, b_ref, rw_ref, rb_ref, h_ref, c_ref):
    y = (DEEPNORM_ALPHA * x_ref[...]
         + jnp.dot(ma_ref[...], woa_ref[...], preferred_element_type=F32)
         + jnp.dot(mb_ref[...], wob_ref[...], preferred_element_type=F32))
    h = _layer_norm(y, g_ref[...], b_ref[...])
    h_ref[...] = h
    logits = jnp.dot(h, rw_ref[...], preferred_element_type=F32, precision=lax.Precision.HIGHEST) + rb_ref[...]
    lane = _lane_iota(logits.shape)
    work = logits
    sel = jnp.zeros(logits.shape, jnp.bool_)
    for _ in range(TOP_K):
        mx = jnp.max(work, axis=-1, keepdims=True)
        idx = jnp.min(jnp.where(work == mx, lane, N_EXPERTS), axis=-1, keepdims=True)
        pick = lane == idx
        sel = sel | pick
        work = jnp.where(pick, -jnp.inf, work)
    e = jnp.where(sel, jnp.exp(logits - jnp.max(logits, axis=-1, keepdims=True)), 0.0)
    c_ref[...] = e / jnp.sum(e, axis=-1, keepdims=True)


def _post_mixer(x, mix_a, mix_b, wo_a, wo_b, g, b, rw, rb, tm):
    n, d = x.shape
    row = lambda w: pl.BlockSpec((tm, w), lambda i: (i, 0))
    full = lambda a: pl.BlockSpec(a.shape, lambda i: (0, 0))
    return pl.pallas_call(
        _post_kernel,
        grid=(n // tm,),
        in_specs=[row(d), row(mix_a.shape[1]), row(mix_b.shape[1]), full(wo_a), full(wo_b), full(g), full(b),
                  full(rw), full(rb)],
        out_specs=[row(d), row(N_EXPERTS)],
        out_shape=[jax.ShapeDtypeStruct((n, d), F32), jax.ShapeDtypeStruct((n, N_EXPERTS), F32)],
        compiler_params=_cparams(("parallel",)),
        name="post_mixer",
    )(x, mix_a, mix_b, wo_a, wo_b, g, b, rw, rb)


def _moe_kernel(cnt_ref, h_ref, comb_ref, upper_ref, wg_ref, bg_ref, wu_ref, bu_ref, wd_ref, bd_ref,
                g_ref, b_ref, o_ref, *, tt, cs, nsub):
    i = pl.program_id(0)
    e = pl.program_id(1)
    n_exp = pl.num_programs(1)

    @pl.when(e == 0)
    def _():
        o_ref[...] = jnp.zeros(o_ref.shape, F32)

    slot0 = lax.broadcasted_iota(I32, (cs, tt), 0).astype(F32) + 1.0

    def sub_body(j, carry):
        n = cnt_ref[(e * pl.num_programs(0) + i) * nsub + j]

        @pl.when(n > 0)
        def _():
            off = pl.multiple_of(j * tt, tt)
            gate = comb_ref[0, 0, pl.ds(j, 1), :]
            sel = gate > 0.0
            self_ = jnp.broadcast_to(jnp.where(sel, 1.0, 0.0), (8, tt)).astype(BF16)
            rank = jnp.dot(self_, upper_ref[...], preferred_element_type=F32)[0:1]
            rank_b = jnp.broadcast_to(rank, (cs, tt))
            sel_b = jnp.broadcast_to(sel, (cs, tt))
            gate_b = jnp.broadcast_to(gate, (cs, tt))
            xj = h_ref[pl.ds(off, tt), :].astype(BF16)

            def chunk_body(c, carry2):
                onehot = sel_b & (rank_b == slot0 + (c * cs).astype(F32))
                pm = jnp.where(onehot, 1.0, 0.0).astype(BF16)
                xs = jnp.dot(pm, xj, preferred_element_type=F32).astype(BF16)
                gg = jnp.dot(xs, wg_ref[0], preferred_element_type=F32) + bg_ref[0]
                uu = jnp.dot(xs, wu_ref[0], preferred_element_type=F32) + bu_ref[0]
                gg = jnp.minimum(gg, SWIGLU_LIMIT)
                uu = jnp.clip(uu, -SWIGLU_LIMIT, SWIGLU_LIMIT)
                hh = gg * jax.nn.sigmoid(SWIGLU_ALPHA * gg) * (uu + 1.0)
                y = jnp.dot(hh.astype(BF16), wd_ref[0], preferred_element_type=F32) + bd_ref[0]
                gs = jnp.sum(jnp.where(onehot, gate_b, 0.0), axis=1, keepdims=True)
                o_ref[pl.ds(off, tt), :] += _tn_dot(pm, (y * gs).astype(BF16))
                return carry2

            lax.fori_loop(0, (n + cs - 1) // cs, chunk_body, 0)

        return carry

    lax.fori_loop(0, nsub, sub_body, 0)

    @pl.when(e == n_exp - 1)
    def _():
        o_ref[...] = _layer_norm(DEEPNORM_ALPHA * h_ref[...] + o_ref[...], g_ref[...], b_ref[...])


def _moe(h, comb, wg, bg, wu, bu, wd, bd, g, b, tb, tt, cs):
    n, d = h.shape
    n_exp, _, f = wg.shape
    nsub = tb // tt
    nblk = n // tb
    comb_t = comb.T
    counts = jnp.sum((comb_t > 0.0).reshape(n_exp * nblk * nsub, tt), axis=-1).astype(I32)
    comb_t = comb_t.reshape(n_exp, nblk, nsub, tt)
    upper = (jnp.arange(tt)[:, None] <= jnp.arange(tt)[None, :]).astype(BF16)
    wspec = lambda a, b_: pl.BlockSpec((1, a, b_), lambda i, e, c: (e, 0, 0))
    vec = pl.BlockSpec((1, d), lambda i, e, c: (0, 0))
    return pl.pallas_call(
        functools.partial(_moe_kernel, tt=tt, cs=cs, nsub=nsub),
        grid_spec=pltpu.PrefetchScalarGridSpec(
            num_scalar_prefetch=1,
            grid=(nblk, n_exp),
            in_specs=[pl.BlockSpec((tb, d), lambda i, e, c: (i, 0)),
                      pl.BlockSpec((1, 1, nsub, tt), lambda i, e, c: (e, i, 0, 0)),
                      pl.BlockSpec((tt, tt), lambda i, e, c: (0, 0)),
                      wspec(d, f), wspec(1, f), wspec(d, f), wspec(1, f), wspec(f, d), wspec(1, d), vec, vec],
            out_specs=pl.BlockSpec((tb, d), lambda i, e, c: (i, 0)),
        ),
        out_shape=jax.ShapeDtypeStruct((n, d), F32),
        compiler_params=_cparams(("parallel", "arbitrary")),
        name="moe",
    )(counts, h, comb_t, upper, wg, bg.reshape(n_exp, 1, f), wu, bu.reshape(n_exp, 1, f),
      wd, bd.reshape(n_exp, 1, d), g, b)


def _pad_w_in(w_in):
    d = w_in.shape[0]
    return jnp.concatenate([w_in, jnp.zeros((d, _IN_COLS_PAD - _IN_COLS), w_in.dtype)], axis=1).astype(BF16)


def _prep_params(p, l=0):
    half = p["w_out"].shape[1] // 2
    vec = lambda a: a[l].reshape(1, -1)
    return {
        "wo_a": p["w_out"][l, :half].astype(BF16), "wo_b": p["w_out"][l, half:].astype(BF16),
        "ln1_g": vec(p["ln1_g"]), "ln1_b": vec(p["ln1_b"]), "ln2_g": vec(p["ln2_g"]), "ln2_b": vec(p["ln2_b"]),
        "router_w": p["router_w"][l], "router_b": vec(p["router_b"]),
        "w_gate": p["w_gate"][l].astype(BF16), "b_gate": p["b_gate"][l],
        "w_up": p["w_up"][l].astype(BF16), "b_up": p["b_up"][l],
        "w_down": p["w_down"][l].astype(BF16), "b_down": p["b_down"][l],
    }


def _pick_tile(n, pref):
    t = min(n, pref)
    while n % t:
        t //= 2
    return t


def _prompt_mixers(x_prompt, w_pad, lam, subln_g, lam_init):
    b, t, d = x_prompt.shape
    n = b * t
    cos, sin = _rope_tables(jnp.arange(t))
    (qa, kaf, kab, vaf, vab, qb, kbf, kbb, vbf, vbb, qi, kif, kib, wi) = _project(
        x_prompt.reshape(n, d), w_pad, cos, sin, _pick_tile(t, 512))
    r3 = lambda a: a.reshape(b, t, a.shape[-1])
    mix_a = _diff_attention(lam, r3(qa), r3(kab), r3(vab), subln_g, lam_init, _pick_tile(t, 256))
    mix_b = _sparse_attention(r3(qi), r3(wi), r3(kib), r3(qb), r3(kbb), r3(vbb), _pick_tile(t, 128),
                              _pick_tile(t, 512))
    rows = (kaf.reshape(1, b, t, A_HEADS, 2 * A_HEAD_DIM), vaf.reshape(1, b, t, A_HEADS, 2 * A_HEAD_DIM),
            kbf.reshape(1, b, t, B_KV_HEADS, B_HEAD_DIM), vbf.reshape(1, b, t, B_KV_HEADS, B_HEAD_DIM),
            kif.reshape(1, b, t, IDX_DIM))
    return mix_a.reshape(n, -1), mix_b.reshape(n, -1), rows


def _decode_mixers(x_sample, w_pad, lam, subln_g, lam_init, caches, page_table, pg):
    ck_diff, cv_diff, ck_sparse, cv_sparse, ck_index = caches
    db, _, d = x_sample.shape
    n_pool, page = ck_diff.shape[:2]
    npg = page_table.shape[1]
    past = npg * page
    cos, sin = _rope_tables(jnp.full((db,), past))
    (qa, kaf, _, vaf, _, qb, kbf, _, vbf, _, qi, kif, _, wi) = _project(x_sample.reshape(db, d), w_pad, cos, sin, db)

    qa3 = qa.reshape(db, A_HEADS, LANES)
    lane = _lane_iota(qa3.shape[1:])[None]
    zero = jnp.zeros_like(qa3)
    qm = jnp.concatenate([jnp.where(lane < HALF, qa3, zero), jnp.where(lane >= HALF, qa3, zero)], axis=1)
    twice = lambda a: jnp.concatenate([a.reshape(db, A_HEADS, LANES)] * 2, axis=1)
    out_a = _dec_diff_attention(page_table, lam, qm, twice(kaf), twice(vaf), subln_g,
                                ck_diff.reshape(n_pool, page * A_HEADS, LANES),
                                cv_diff.reshape(n_pool, page * A_HEADS, LANES), lam_init, pg)
    mix_a = out_a[:, :A_HEADS].reshape(db, A_HEADS * LANES).astype(BF16)

    scores, sc_new = _dec_index_scores(page_table, qi.reshape(db, IDX_HEADS, IDX_DIM), wi.reshape(db, IDX_HEADS, 1),
                                       kif.reshape(db, 1, IDX_DIM), jnp.transpose(ck_index, (0, 2, 1)), pg)
    sc_all = jnp.concatenate([scores[:, 0], sc_new[:, 0, :1], jnp.zeros((db, LANES - 1), F32)], axis=1)
    keys, tau = _dec_select(sc_all, past + 1, min(TOPK_MAX, (past + 1) // 4))

    grp = B_HEADS // B_KV_HEADS
    qb4 = qb.reshape(db, B_KV_HEADS, grp, B_HEAD_DIM)
    qs = jnp.concatenate([qb4, jnp.zeros((db, B_KV_HEADS, 8 - grp, B_HEAD_DIM), qb.dtype)], axis=2)
    out_b = _dec_sparse_attention(page_table, qs, keys.reshape(db, 1, -1), tau.reshape(db, 1, LANES),
                                  kbf.reshape(db, B_KV_HEADS, 1, B_HEAD_DIM), vbf.reshape(db, B_KV_HEADS, 1, B_HEAD_DIM),
                                  jnp.transpose(ck_sparse, (0, 2, 3, 1)), jnp.transpose(cv_sparse, (0, 2, 3, 1)), pg)
    mix_b = out_b[:, :, :grp].reshape(db, B_HEADS * B_HEAD_DIM).astype(BF16)
    rows = (kaf.reshape(1, db, 1, A_HEADS, 2 * A_HEAD_DIM), vaf.reshape(1, db, 1, A_HEADS, 2 * A_HEAD_DIM),
            kbf.reshape(1, db, 1, B_KV_HEADS, B_HEAD_DIM), vbf.reshape(1, db, 1, B_KV_HEADS, B_HEAD_DIM),
            kif.reshape(1, db, 1, IDX_DIM))
    return mix_a, mix_b, rows


def _post_layer(x, mix_a, mix_b, p, tb, tt):
    n, d = x.shape
    h, comb = _post_mixer(x, mix_a, mix_b, p["wo_a"], p["wo_b"], p["ln1_g"], p["ln1_b"], p["router_w"],
                          p["router_b"], _pick_tile(n, 512))
    return _moe(h, comb, p["w_gate"], p["b_gate"], p["w_up"], p["b_up"], p["w_down"], p["b_down"],
                p["ln2_g"], p["ln2_b"], tb, tt, min(tt, 128))


def kernel(x_prompt, x_sample, cache_k_diff, cache_v_diff, cache_k_sparse, cache_v_sparse, cache_k_index, page_table,
           w_in, lambda_q1, lambda_k1, lambda_q2, lambda_k2, subln_g, w_out, ln1_g, ln1_b, router_w, router_b,
           w_gate, b_gate, w_up, b_up, w_down, b_down, ln2_g, ln2_b):
    assert w_in.shape[0] == DEPTH
    l = 0
    b, t, d = x_prompt.shape
    db = x_sample.shape[0]
    lam_init = 0.8 - 0.6 * math.exp(-0.3 * l)
    lam = (jnp.exp(jnp.sum(lambda_q1[l] * lambda_k1[l])) - jnp.exp(jnp.sum(lambda_q2[l] * lambda_k2[l]))
           + lam_init).reshape(1, 1)
    g = subln_g[l].reshape(1, -1)
    w_pad = _pad_w_in(w_in[l])
    params = _prep_params(dict(w_out=w_out, ln1_g=ln1_g, ln1_b=ln1_b, router_w=router_w, router_b=router_b,
                               w_gate=w_gate, b_gate=b_gate, w_up=w_up, b_up=b_up, w_down=w_down, b_down=b_down,
                               ln2_g=ln2_g, ln2_b=ln2_b), l)

    mix_a, mix_b, rows_p = _prompt_mixers(x_prompt, w_pad, lam, g, lam_init)
    n = b * t
    y_p = _post_layer(x_prompt.reshape(n, d), mix_a, mix_b, params, _pick_tile(n, 2048), _pick_tile(n, 512))

    caches = (cache_k_diff[l], cache_v_diff[l], cache_k_sparse[l], cache_v_sparse[l], cache_k_index[l])
    mix_a, mix_b, rows_s = _decode_mixers(x_sample, w_pad, lam, g, lam_init, caches, page_table,
                                          _pick_tile(page_table.shape[1], 8))
    y_s = _post_layer(x_sample.reshape(db, d), mix_a, mix_b, params, db, db)
    return (y_p.reshape(b, t, d), y_s.reshape(db, 1, d)) + rows_p + rows_s
```

```python
import functools
import math

import jax
import jax.numpy as jnp
from jax import lax
from jax.experimental import pallas as pl
from jax.experimental.pallas import tpu as pltpu

F32 = jnp.float32
BF16 = jnp.bfloat16
I32 = jnp.int32

A_HEADS = 4
A_HEAD_DIM = 64
B_HEADS = 8
B_KV_HEADS = 2
B_HEAD_DIM = 64
IDX_HEADS = 8
IDX_DIM = 64
TOPK_MAX = 256
N_EXPERTS = 32
TOP_K = 4
SWIGLU_LIMIT = 7.0
SWIGLU_ALPHA = 1.702
ROPE_THETA = 10000.0
LN_EPS = 1e-5
RMS_EPS = 1e-5
DEPTH = 1
DEEPNORM_ALPHA = (2.0 * DEPTH) ** 0.25
IDX_SCALE = (IDX_HEADS * IDX_DIM) ** -0.5

LANES = 128
HALF = 64
NEG_BIG = -1e30
INT_MIN = -(2 ** 31)
VMEM_LIMIT = 56 * 1024 * 1024

_C_QA, _C_KA, _C_VA, _C_QB, _C_KB, _C_VB, _C_QI, _C_KIWI = 0, 512, 1024, 1536, 2048, 2176, 2304, 2816
_IN_COLS = 2888
_IN_COLS_PAD = 2944


def _cparams(sem):
    return pltpu.CompilerParams(dimension_semantics=sem, vmem_limit_bytes=VMEM_LIMIT)


def _nt_dot(a, b):
    return lax.dot_general(a, b, (((1,), (1,)), ((), ())), preferred_element_type=F32)


def _tn_dot(a, b):
    return lax.dot_general(a, b, (((0,), (0,)), ((), ())), preferred_element_type=F32)


def _lane_iota(shape):
    return lax.broadcasted_iota(I32, shape, 1)


def _proj_kernel(x_ref, w_ref, cos_ref, sin_ref,
                 qa_ref, kaf_ref, kab_ref, vaf_ref, vab_ref, qb_ref, kbf_ref, kbb_ref, vbf_ref, vbb_ref,
                 qi_ref, kif_ref, kib_ref, wi_ref):
    xb = x_ref[...].astype(BF16)
    cos = cos_ref[...]
    sin = sin_ref[...]
    lane = _lane_iota(cos.shape)
    first = (lane % HALF) < (HALF // 2)

    def seg(c0, n):
        return jnp.dot(xb, w_ref[:, c0:c0 + n], preferred_element_type=F32)

    def rope(h):
        rot = jnp.where(first, pltpu.roll(h, LANES - HALF // 2, 1), pltpu.roll(h, HALF // 2, 1))
        return h * cos + rot * sin

    def blocks(h):
        return [h[:, k * LANES:(k + 1) * LANES] for k in range(h.shape[1] // LANES)]

    scale = A_HEAD_DIM ** -0.5
    for k, h in enumerate(blocks(seg(_C_QA, 512))):
        qa_ref[:, k * LANES:(k + 1) * LANES] = (rope(h) * scale).astype(BF16)
    for k, h in enumerate(blocks(seg(_C_KA, 512))):
        r = rope(h)
        kaf_ref[:, k * LANES:(k + 1) * LANES] = r
        kab_ref[:, k * LANES:(k + 1) * LANES] = r.astype(BF16)
    va = seg(_C_VA, 512)
    vaf_ref[...] = va
    vab_ref[...] = va.astype(BF16)
    for k, h in enumerate(blocks(seg(_C_QB, 512))):
        qb_ref[:, k * LANES:(k + 1) * LANES] = (rope(h) * scale).astype(BF16)
    kbvb = seg(_C_KB, 256)
    kb = rope(kbvb[:, :LANES])
    kbf_ref[...] = kb
    kbb_ref[...] = kb.astype(BF16)
    vb = kbvb[:, LANES:]
    vbf_ref[...] = vb
    vbb_ref[...] = vb.astype(BF16)
    for k, h in enumerate(blocks(seg(_C_QI, 512))):
        qi_ref[:, k * LANES:(k + 1) * LANES] = rope(h).astype(BF16)
    kiwi = seg(_C_KIWI, LANES)
    ki = rope(kiwi)
    kif_ref[...] = ki[:, :IDX_DIM]
    kib_ref[...] = jnp.where(lane < HALF, ki, pltpu.roll(ki, HALF, 1)).astype(BF16)
    wi_ref[...] = pltpu.roll(kiwi, HALF, 1)[:, :IDX_HEADS]


def _project(x, w_pad, cos, sin, tm):
    n, d = x.shape
    t_tab = cos.shape[0]
    n_tab = t_tab // tm
    row = lambda i: (i, 0)
    shapes = [((n, 512), BF16), ((n, 512), F32), ((n, 512), BF16), ((n, 512), F32), ((n, 512), BF16),
              ((n, 512), BF16), ((n, LANES), F32), ((n, LANES), BF16), ((n, LANES), F32), ((n, LANES), BF16),
              ((n, 512), BF16), ((n, IDX_DIM), F32), ((n, LANES), BF16), ((n, IDX_HEADS), F32)]
    return pl.pallas_call(
        _proj_kernel,
        grid=(n // tm,),
        in_specs=[pl.BlockSpec((tm, d), row),
                  pl.BlockSpec((d, _IN_COLS_PAD), lambda i: (0, 0)),
                  pl.BlockSpec((tm, LANES), lambda i: (i % n_tab, 0)),
                  pl.BlockSpec((tm, LANES), lambda i: (i % n_tab, 0))],
        out_specs=[pl.BlockSpec((tm, s[1]), row) for s, _ in shapes],
        out_shape=[jax.ShapeDtypeStruct(s, dt) for s, dt in shapes],
        compiler_params=_cparams(("parallel",)),
        name="proj",
    )(x, w_pad, cos, sin)


def _rope_tables(pos):
    inv = ROPE_THETA ** (-jnp.arange(0, HALF, 2, dtype=F32) / HALF)
    ang = pos.astype(F32)[:, None] * inv[None, :]
    cos = jnp.concatenate([jnp.cos(ang)] * 4, axis=-1)
    sin = jnp.concatenate([-jnp.sin(ang), jnp.sin(ang)] * 2, axis=-1)
    return cos, sin


def _diff_finish(acc, l, lam, g, lam_init, rows):
    o = acc[:rows] / l[:rows] - lam * (acc[rows:] / l[rows:])
    o = o * lax.rsqrt(jnp.mean(o * o, axis=-1, keepdims=True) + RMS_EPS)
    return o * g * (1.0 - lam_init)


ONES_ROWS = 16


def _with_ones_rows(vt):
    return jnp.concatenate([vt, jnp.ones(vt.shape[:-2] + (ONES_ROWS, vt.shape[-1]), vt.dtype)], axis=-2)


def _diff_kernel(lam_ref, q_ref, k_ref, vt_ref, g_ref, o_ref, m_scr, acc_scr, *, tq, lam_init):
    qi = pl.program_id(2)
    q = q_ref[0]
    lane = _lane_iota(q.shape)
    zero = jnp.zeros_like(q)
    qq = jnp.concatenate([jnp.where(lane < HALF, q, zero), jnp.where(lane >= HALF, q, zero)], axis=0)
    m_scr[...] = jnp.full(m_scr.shape, NEG_BIG, F32)
    acc_scr[...] = jnp.zeros(acc_scr.shape, F32)

    def step(j, diagonal):
        off = pl.multiple_of(j * tq, tq)
        s = _nt_dot(k_ref[0, pl.ds(off, tq), :], qq)
        if diagonal:
            kpos = lax.broadcasted_iota(I32, s.shape, 0)
            qpos = lax.broadcasted_iota(I32, s.shape, 1) % tq
            s = jnp.where(kpos <= qpos, s, NEG_BIG)
        m_old = m_scr[...]
        m_new = jnp.maximum(m_old, jnp.max(s, axis=0, keepdims=True))
        a = jnp.exp(m_old - m_new)
        p = jnp.exp(s - m_new)
        acc_scr[...] = a * acc_scr[...] + jnp.dot(vt_ref[0, 0, j], p.astype(BF16), preferred_element_type=F32)
        m_scr[...] = m_new

    def body(j, carry):
        step(j, False)
        return carry

    lax.fori_loop(0, qi, body, 0)
    step(qi, True)
    o = acc_scr[:LANES, :] * (1.0 / acc_scr[LANES:LANES + 1, :])
    o = o[:, :tq] - lam_ref[0, 0] * o[:, tq:]
    o = o * lax.rsqrt(jnp.mean(o * o, axis=0, keepdims=True) + RMS_EPS)
    o = o * g_ref[...] * (1.0 - lam_init)
    o_ref[0] = o.T.astype(o_ref.dtype)


def _diff_attention(lam, qa, ka, va, g, lam_init, tq):
    b, t, _ = qa.shape
    nblk = t // tq
    vt = _with_ones_rows(jnp.transpose(va.reshape(b, nblk, tq, A_HEADS, LANES), (0, 3, 1, 4, 2)))
    rows = LANES + ONES_ROWS
    return pl.pallas_call(
        functools.partial(_diff_kernel, tq=tq, lam_init=lam_init),
        grid=(b, A_HEADS, nblk),
        in_specs=[pl.BlockSpec(memory_space=pltpu.SMEM),
                  pl.BlockSpec((1, tq, LANES), lambda b_, h, i: (b_, i, h)),
                  pl.BlockSpec((1, t, LANES), lambda b_, h, i: (b_, 0, h)),
                  pl.BlockSpec((1, 1, nblk, rows, tq), lambda b_, h, i: (b_, h, 0, 0, 0)),
                  pl.BlockSpec((LANES, 1), lambda b_, h, i: (0, 0))],
        out_specs=pl.BlockSpec((1, tq, LANES), lambda b_, h, i: (b_, i, h)),
        out_shape=jax.ShapeDtypeStruct((b, t, A_HEADS * LANES), BF16),
        scratch_shapes=[pltpu.VMEM((1, 2 * tq), F32), pltpu.VMEM((rows, 2 * tq), F32)],
        compiler_params=_cparams(("parallel", "parallel", "arbitrary")),
        name="diff_attn",
    )(lam, qa, ka, vt, g.reshape(LANES, 1))


def _float_key(x):
    b = lax.bitcast_convert_type(x + 0.0, I32)
    return b ^ ((b >> 31) & 0x7FFFFFFF)


def _select_topk(key_scr, nch, ck, topk):
    nq = key_scr.shape[1]
    kf = float(topk)

    grp = 64

    def counts(tau_b, strict):
        def body(c, cnt):
            off = pl.multiple_of(c * ck, ck)
            for g in range(ck // grp):
                kk = key_scr[pl.ds(off + g * grp, grp), :]
                hit = (kk > tau_b) if strict else (kk >= tau_b)
                cnt = cnt + jnp.where(hit, 1, 0)
            return cnt
        cnt = lax.fori_loop(0, nch, body, jnp.zeros((grp, nq), I32))
        return jnp.sum(cnt.astype(F32), axis=0, keepdims=True)

    def bcast(v, rows=grp):
        return jnp.broadcast_to(v, (rows, nq))

    cnt0 = counts(jnp.zeros((grp, nq), I32), False)
    tau = jnp.where(cnt0 >= kf, 0, INT_MIN).astype(I32)

    def bit_body(i, tau):
        cand = tau | jnp.left_shift(jnp.int32(1), 30 - i)
        cnt = counts(bcast(cand), False)
        return jnp.where(cnt >= kf, cand, tau)

    tau = lax.fori_loop(0, 31, bit_body, tau)
    cnt_ge = counts(bcast(tau), False)
    cnt_gt = counts(bcast(tau), True)
    overfull = jnp.where((cnt_ge > kf) & (tau > INT_MIN), 1.0, 0.0)

    @pl.when(jnp.max(overfull) > 0.0)
    def _():
        tau_b = bcast(tau, LANES)
        need_b = bcast(kf - cnt_gt, LANES)
        lower = jnp.where(lax.broadcasted_iota(I32, (LANES, LANES), 0)
                          >= lax.broadcasted_iota(I32, (LANES, LANES), 1), 1.0, 0.0).astype(BF16)

        def body(c, run):
            off = pl.multiple_of(c * ck, ck)
            for g in range(ck // LANES):
                sl = pl.ds(off + g * LANES, LANES)
                kk = key_scr[sl, :]
                eq = kk == tau_b
                eqf = jnp.where(eq, 1.0, 0.0)
                pre = jnp.dot(lower, eqf.astype(BF16), preferred_element_type=F32)
                drop = eq & ((pre + run) > need_b)
                key_scr[sl, :] = jnp.where(drop, INT_MIN, kk)
                run = run + jnp.sum(eqf, axis=0, keepdims=True)
            return run

        lax.fori_loop(0, nch, body, jnp.zeros((1, nq), F32))

    return tau


def _stack_heads(q, n_heads):
    lane = _lane_iota((q.shape[0], LANES))
    zero = jnp.zeros((q.shape[0], LANES), q.dtype)
    parts = []
    for h in range(n_heads):
        blk = q[:, (h // 2) * LANES:(h // 2 + 1) * LANES]
        parts.append(jnp.where((lane < HALF) == (h % 2 == 0), blk, zero))
    return jnp.concatenate(parts, axis=0)


M_INIT = -1e29


def _sparse_kernel(qi_ref, wt_ref, ki_ref, qb_ref, kb_ref, vt_ref, o_ref, key_scr, m_scr, acc_scr,
                   *, tq, ck, topk):
    i = pl.program_id(1)
    nch = ((i + 1) * tq + ck - 1) // ck
    kpos0 = lax.broadcasted_iota(I32, (ck, tq), 0)
    qpos = i * tq + lax.broadcasted_iota(I32, (ck, tq), 1)

    qs = _stack_heads(qi_ref[0], IDX_HEADS)
    wt = wt_ref[0]

    def score_body(c, carry):
        off = pl.multiple_of(c * ck, ck)
        d = _nt_dot(ki_ref[0, pl.ds(off, ck), :], qs)
        sc = jnp.zeros((ck, tq), F32)
        for h in range(IDX_HEADS):
            sc = sc + jnp.maximum(d[:, h * tq:(h + 1) * tq], 0.0) * wt[h:h + 1, :]
        sc = sc * IDX_SCALE
        key_scr[pl.ds(off, ck), :] = jnp.where(kpos0 + off <= qpos, _float_key(sc), INT_MIN)
        return carry

    lax.fori_loop(0, nch, score_body, 0)
    tau = _select_topk(key_scr, nch, ck, topk)
    tau_b = jnp.broadcast_to(tau, (ck, tq))

    qh = []
    qb = qb_ref[0]
    lane = _lane_iota((tq, LANES))
    zero = jnp.zeros((tq, LANES), qb.dtype)
    grp = B_HEADS // B_KV_HEADS
    for h in range(B_HEADS):
        blk = qb[:, (h // 2) * LANES:(h // 2 + 1) * LANES]
        if h % 2 != h // grp:
            blk = pltpu.roll(blk, HALF, 1)
        qh.append(jnp.where((lane >= HALF) == (h // grp == 1), blk, zero))
    qs2 = jnp.concatenate(qh, axis=0)
    m_scr[...] = jnp.full(m_scr.shape, M_INIT, F32)
    acc_scr[...] = jnp.zeros(acc_scr.shape, F32)

    def attn_body(c, carry):
        off = pl.multiple_of(c * ck, ck)
        sel = (key_scr[pl.ds(off, ck), :] >= tau_b) & (kpos0 + off <= qpos)
        bias = jnp.where(sel, 0.0, NEG_BIG)
        s = _nt_dot(kb_ref[0, pl.ds(off, ck), :], qs2)
        ps, scales = [], []
        for h in range(B_HEADS):
            hs = slice(h * tq, (h + 1) * tq)
            sh = s[:, hs] + bias
            m_old = m_scr[:, hs]
            m_new = jnp.maximum(m_old, jnp.max(sh, axis=0, keepdims=True))
            a = jnp.exp(m_old - m_new)
            p = jnp.exp(sh - m_new)
            m_scr[:, hs] = m_new
            ps.append(p.astype(BF16))
            scales.append(a)
        pv = jnp.dot(vt_ref[0, c], jnp.concatenate(ps, axis=1), preferred_element_type=F32)
        acc_scr[...] = jnp.concatenate(scales, axis=1) * acc_scr[...] + pv
        return carry

    lax.fori_loop(0, nch, attn_body, 0)
    o = acc_scr[:LANES, :] * (1.0 / acc_scr[LANES:LANES + 1, :])
    parts = [o[(h // grp) * HALF:(h // grp + 1) * HALF, h * tq:(h + 1) * tq] for h in range(B_HEADS)]
    o_ref[0] = jnp.concatenate(parts, axis=0).T.astype(o_ref.dtype)


def _sparse_attention(qi, wi, ki2, qb, kb, vb, tq, ck):
    b, t, _ = qi.shape
    topk = min(TOPK_MAX, t // 4)
    wt = jnp.transpose(wi, (0, 2, 1))
    vt = _with_ones_rows(jnp.transpose(vb.reshape(b, t // ck, ck, LANES), (0, 1, 3, 2)))
    rows = LANES + ONES_ROWS
    blk_q = lambda w: pl.BlockSpec((1, tq, w), lambda b_, i: (b_, i, 0))
    blk_k = pl.BlockSpec((1, t, LANES), lambda b_, i: (b_, 0, 0))
    return pl.pallas_call(
        functools.partial(_sparse_kernel, tq=tq, ck=ck, topk=topk),
        grid=(b, t // tq),
        in_specs=[blk_q(512), pl.BlockSpec((1, IDX_HEADS, tq), lambda b_, i: (b_, 0, i)), blk_k, blk_q(512), blk_k,
                  pl.BlockSpec((1, t // ck, rows, ck), lambda b_, i: (b_, 0, 0, 0))],
        out_specs=blk_q(512),
        out_shape=jax.ShapeDtypeStruct((b, t, 512), BF16),
        scratch_shapes=[pltpu.VMEM((t, tq), I32), pltpu.VMEM((1, B_HEADS * tq), F32),
                        pltpu.VMEM((rows, B_HEADS * tq), F32)],
        compiler_params=_cparams(("parallel", "arbitrary")),
        name="sparse_attn",
    )(qi, wt, ki2, qb, kb, vt)


def _page_specs(n, block, pg):
    zeros = (0,) * (len(block) - 1)
    return [pl.BlockSpec(block, functools.partial(lambda i, b, j, pt: (pt[b, j * pg + i],) + zeros, i))
            for i in range(n)]


def _dec_diff_kernel(pt_ref, lam_ref, q_ref, kn_ref, vn_ref, g_ref, qi_ref, w_ref, kin_ref, *rest, pg, lam_init):
    k_refs, v_refs, ki_refs = rest[:pg], rest[pg:2 * pg], rest[2 * pg:3 * pg]
    o_ref, sc_ref, new_ref, m_scr, l_scr, acc_scr, kbuf, vbuf, kibuf = rest[3 * pg:]
    j = pl.program_id(1)

    @pl.when(j == 0)
    def _():
        m_scr[...] = jnp.full(m_scr.shape, NEG_BIG, F32)
        l_scr[...] = jnp.zeros(l_scr.shape, F32)
        acc_scr[...] = jnp.zeros(acc_scr.shape, F32)

    rows_pg = k_refs[0].shape[1]
    page = ki_refs[0].shape[2]
    for i in range(pg):
        kbuf[i * rows_pg:(i + 1) * rows_pg, :] = k_refs[i][0].astype(BF16)
        vbuf[i * rows_pg:(i + 1) * rows_pg, :] = v_refs[i][0].astype(BF16)
        kibuf[:, i * page:(i + 1) * page] = ki_refs[i][0].astype(BF16)

    q = q_ref[0]
    s = _nt_dot(q, kbuf[...])
    same_head = (_lane_iota(s.shape) % A_HEADS) == (lax.broadcasted_iota(I32, s.shape, 0) % A_HEADS)
    s = jnp.where(same_head, s, NEG_BIG)
    m_old = m_scr[...]
    m_new = jnp.maximum(m_old, jnp.max(s, axis=1, keepdims=True))
    a = jnp.exp(m_old - m_new)
    p = jnp.exp(s - m_new).astype(BF16)
    l_scr[...] = a * l_scr[...] + jnp.sum(p.astype(F32), axis=1, keepdims=True)
    acc_scr[...] = a * acc_scr[...] + jnp.dot(p, vbuf[...], preferred_element_type=F32)
    m_scr[...] = m_new

    qi = qi_ref[0]
    w = w_ref[0]
    d = jnp.dot(qi, kibuf[...], preferred_element_type=F32)
    sc_ref[0] = jnp.sum(jnp.maximum(d, 0.0) * w, axis=0, keepdims=True) * IDX_SCALE

    @pl.when(j == pl.num_programs(1) - 1)
    def _():
        s_new = jnp.sum(q.astype(F32) * kn_ref[0], axis=1, keepdims=True)
        m_old = m_scr[...]
        m_new = jnp.maximum(m_old, s_new)
        a = jnp.exp(m_old - m_new)
        pn = jnp.exp(s_new - m_new)
        l = a * l_scr[...] + pn
        acc = a * acc_scr[...] + pn * vn_ref[0]
        o = _diff_finish(acc, l, lam_ref[0, 0], g_ref[...], lam_init, A_HEADS)
        o_ref[0] = jnp.concatenate([o, jnp.zeros_like(o)], axis=0)
        dn = jnp.sum(qi.astype(F32) * kin_ref[0], axis=1, keepdims=True)
        scn = jnp.sum(jnp.maximum(dn, 0.0) * w, axis=0, keepdims=True) * IDX_SCALE
        new_ref[0] = jnp.broadcast_to(scn, new_ref.shape[1:])


def _dec_diff_attention(pt, lam, qm, knew, vnew, g, qi, wi, kinew, kpages, vpages, kipages, lam_init, pg):
    db, npg = pt.shape
    rows_pg = kpages.shape[1]
    page = kipages.shape[2]
    per_b = lambda shape: pl.BlockSpec(shape, lambda b, j, pt_: (b,) + (0,) * (len(shape) - 1))
    return pl.pallas_call(
        functools.partial(_dec_diff_kernel, pg=pg, lam_init=lam_init),
        grid_spec=pltpu.PrefetchScalarGridSpec(
            num_scalar_prefetch=1,
            grid=(db, npg // pg),
            in_specs=[pl.BlockSpec(memory_space=pltpu.SMEM), per_b((1, 8, LANES)), per_b((1, 8, LANES)),
                      per_b((1, 8, LANES)), pl.BlockSpec((1, LANES), lambda b, j, pt_: (0, 0)),
                      per_b((1, IDX_HEADS, IDX_DIM)), per_b((1, IDX_HEADS, 1)), per_b((1, 1, IDX_DIM))]
                     + _page_specs(pg, (1,) + kpages.shape[1:], pg) + _page_specs(pg, (1,) + vpages.shape[1:], pg)
                     + _page_specs(pg, (1,) + kipages.shape[1:], pg),
            out_specs=[per_b((1, 8, LANES)), pl.BlockSpec((1, 1, pg * page), lambda b, j, pt_: (b, 0, j)),
                       per_b((1, 1, LANES))],
            scratch_shapes=[pltpu.VMEM((8, 1), F32), pltpu.VMEM((8, 1), F32), pltpu.VMEM((8, LANES), F32),
                            pltpu.VMEM((pg * rows_pg, LANES), BF16), pltpu.VMEM((pg * rows_pg, LANES), BF16),
                            pltpu.VMEM((IDX_DIM, pg * page), BF16)],
        ),
        out_shape=[jax.ShapeDtypeStruct((db, 8, LANES), F32), jax.ShapeDtypeStruct((db, 1, npg * page), F32),
                   jax.ShapeDtypeStruct((db, 1, LANES), F32)],
        compiler_params=_cparams(("parallel", "arbitrary")),
        name="dec_diff_attn",
    )(pt, lam, qm, knew, vnew, g, qi, wi, kinew, *([kpages] * pg), *([vpages] * pg), *([kipages] * pg))


def _dec_select_kernel(sc_ref, key_ref, tau_ref, *, n_keys, topk):
    width, nq = sc_ref.shape
    kpos = lax.broadcasted_iota(I32, (width, nq), 0)
    key_ref[...] = jnp.where(kpos < n_keys, _float_key(sc_ref[...]), INT_MIN)
    tau = _select_topk(key_ref, width // LANES, LANES, topk)
    tau_ref[...] = jnp.broadcast_to(tau, tau_ref.shape)


def _dec_select(scores_t, n_keys, topk):
    width, nq = scores_t.shape
    return pl.pallas_call(
        functools.partial(_dec_select_kernel, n_keys=n_keys, topk=topk),
        out_shape=[jax.ShapeDtypeStruct((width, nq), I32), jax.ShapeDtypeStruct((8, nq), I32)],
        compiler_params=pltpu.CompilerParams(vmem_limit_bytes=VMEM_LIMIT),
        name="dec_select",
    )(scores_t)


def _dec_sparse_kernel(pt_ref, q_ref, key_ref, keyn_ref, tau_ref, kn_ref, vn_ref, *rest, pg):
    k_refs, v_refs = rest[:pg], rest[pg:2 * pg]
    o_ref, m_scr, l_scr, acc_scr, kbuf, vbuf = rest[2 * pg:]
    j = pl.program_id(1)

    @pl.when(j == 0)
    def _():
        m_scr[...] = jnp.full(m_scr.shape, NEG_BIG, F32)
        l_scr[...] = jnp.zeros(l_scr.shape, F32)
        acc_scr[...] = jnp.zeros(acc_scr.shape, F32)

    page = k_refs[0].shape[3]
    for i in range(pg):
        kbuf[:, :, i * page:(i + 1) * page] = k_refs[i][0].astype(BF16)
        vbuf[:, :, i * page:(i + 1) * page] = v_refs[i][0].astype(BF16)
    tau = tau_ref[0][:, 0:1]
    sel = jnp.broadcast_to(key_ref[0] >= tau, (8, pg * page))
    for g in range(B_KV_HEADS):
        q = q_ref[0, g]
        s = jnp.dot(q, kbuf[g], preferred_element_type=F32)
        s = jnp.where(sel, s, NEG_BIG)
        m_old = m_scr[g]
        m_new = jnp.maximum(m_old, jnp.max(s, axis=1, keepdims=True))
        a = jnp.exp(m_old - m_new)
        p = jnp.where(sel, jnp.exp(s - m_new), 0.0).astype(BF16)
        l_scr[g] = a * l_scr[g] + jnp.sum(p.astype(F32), axis=1, keepdims=True)
        acc_scr[g] = a * acc_scr[g] + _nt_dot(p, vbuf[g])
        m_scr[g] = m_new

    @pl.when(j == pl.num_programs(1) - 1)
    def _():
        sel_new = keyn_ref[0][:, 0:1] >= tau
        for g in range(B_KV_HEADS):
            s_new = jnp.sum(q_ref[0, g].astype(F32) * kn_ref[0, g], axis=1, keepdims=True)
            s_new = jnp.where(sel_new, s_new, NEG_BIG)
            m_old = m_scr[g]
            m_new = jnp.maximum(m_old, s_new)
            a = jnp.exp(m_old - m_new)
            pn = jnp.where(sel_new, jnp.exp(s_new - m_new), 0.0)
            l = a * l_scr[g] + pn
            o_ref[0, g] = (a * acc_scr[g] + pn * vn_ref[0, g]) / l


def _dec_sparse_attention(pt, q, keys, tau, kbnew, vbnew, kpages, vpages, pg):
    db, npg = pt.shape
    page = kpages.shape[3]
    per_b = lambda shape: pl.BlockSpec(shape, lambda b, j, pt_: (b,) + (0,) * (len(shape) - 1))
    blk = (1,) + kpages.shape[1:]
    return pl.pallas_call(
        functools.partial(_dec_sparse_kernel, pg=pg),
        grid_spec=pltpu.PrefetchScalarGridSpec(
            num_scalar_prefetch=1,
            grid=(db, npg // pg),
            in_specs=[per_b((1, B_KV_HEADS, 8, B_HEAD_DIM)),
                      pl.BlockSpec((1, 1, pg * page), lambda b, j, pt_: (b, 0, j)),
                      pl.BlockSpec((1, 1, LANES), lambda b, j, pt_: (b, 0, npg * page // LANES)),
                      per_b((1, 1, LANES)), per_b((1, B_KV_HEADS, 1, B_HEAD_DIM)),
                      per_b((1, B_KV_HEADS, 1, B_HEAD_DIM))]
                     + _page_specs(pg, blk, pg) + _page_specs(pg, blk, pg),
            out_specs=per_b((1, B_KV_HEADS, 8, B_HEAD_DIM)),
            scratch_shapes=[pltpu.VMEM((B_KV_HEADS, 8, 1), F32), pltpu.VMEM((B_KV_HEADS, 8, 1), F32),
                            pltpu.VMEM((B_KV_HEADS, 8, B_HEAD_DIM), F32),
                            pltpu.VMEM((B_KV_HEADS, B_HEAD_DIM, pg * page), BF16),
                            pltpu.VMEM((B_KV_HEADS, B_HEAD_DIM, pg * page), BF16)],
        ),
        out_shape=jax.ShapeDtypeStruct((db, B_KV_HEADS, 8, B_HEAD_DIM), F32),
        compiler_params=_cparams(("parallel", "arbitrary")),
        name="dec_sparse_attn",
    )(pt, q, keys, keys, tau, kbnew, vbnew, *([kpages] * pg), *([vpages] * pg))


def _layer_norm(x, g, b):
    mu = jnp.mean(x, axis=-1, keepdims=True)
    xc = x - mu
    var = jnp.mean(xc * xc, axis=-1, keepdims=True)
    return xc * lax.rsqrt(var + LN_EPS) * g + b


def _post_kernel(x_ref, ma_ref, mb_ref, woa_ref, wob_ref, g_ref, b_ref, rw_ref, rb_ref, h_ref, c_ref):
    y = (DEEPNORM_ALPHA * x_ref[...]
         + jnp.dot(ma_ref[...], woa_ref[...], preferred_element_type=F32)
         + jnp.dot(mb_ref[...], wob_ref[...], preferred_element_type=F32))
    h = _layer_norm(y, g_ref[...], b_ref[...])
    h_ref[...] = h
    logits = jnp.dot(h, rw_ref[...], preferred_element_type=F32, precision=lax.Precision.HIGHEST) + rb_ref[...]
    lane = _lane_iota(logits.shape)
    work = logits
    sel = jnp.zeros(logits.shape, jnp.bool_)
    for _ in range(TOP_K):
        mx = jnp.max(work, axis=-1, keepdims=True)
        idx = jnp.min(jnp.where(work == mx, lane, N_EXPERTS), axis=-1, keepdims=True)
        pick = lane == idx
        sel = sel | pick
        work = jnp.where(pick, -jnp.inf, work)
    e = jnp.where(sel, jnp.exp(logits - jnp.max(logits, axis=-1, keepdims=True)), 0.0)
    c_ref[...] = e / jnp.sum(e, axis=-1, keepdims=True)


def _post_mixer(x, mix_a, mix_b, wo_a, wo_b, g, b, rw, rb, tm):
    n, d = x.shape
    row = lambda w: pl.BlockSpec((tm, w), lambda i: (i, 0))
    full = lambda a: pl.BlockSpec(a.shape, lambda i: (0, 0))
    return pl.pallas_call(
        _post_kernel,
        grid=(n // tm,),
        in_specs=[row(d), row(mix_a.shape[1]), row(mix_b.shape[1]), full(wo_a), full(wo_b), full(g), full(b),
                  full(rw), full(rb)],
        out_specs=[row(d), row(N_EXPERTS)],
        out_shape=[jax.ShapeDtypeStruct((n, d), F32), jax.ShapeDtypeStruct((n, N_EXPERTS), F32)],
        compiler_params=_cparams(("parallel",)),
        name="post_mixer",
    )(x, mix_a, mix_b, wo_a, wo_b, g, b, rw, rb)


def _moe_kernel(cnt_ref, h_ref, comb_ref, upper_ref, wg_ref, bg_ref, wu_ref, bu_ref, wd_ref, bd_ref,
                g_ref, b_ref, o_ref, *, tt, cs, nsub):
    i = pl.program_id(0)
    e = pl.program_id(1)
    n_exp = pl.num_programs(1)

    @pl.when(e == 0)
    def _():
        o_ref[...] = jnp.zeros(o_ref.shape, F32)

    slot0 = lax.broadcasted_iota(I32, (cs, tt), 0).astype(F32) + 1.0

    def sub_body(j, carry):
        n = cnt_ref[(e * pl.num_programs(0) + i) * nsub + j]

        @pl.when(n > 0)
        def _():
            off = pl.multiple_of(j * tt, tt)
            gate = comb_ref[0, 0, pl.ds(j, 1), :]
            sel = gate > 0.0
            self_ = jnp.broadcast_to(jnp.where(sel, 1.0, 0.0), (8, tt)).astype(BF16)
            rank = jnp.dot(self_, upper_ref[...], preferred_element_type=F32)[0:1]
            rank_b = jnp.broadcast_to(rank, (cs, tt))
            sel_b = jnp.broadcast_to(sel, (cs, tt))
            gate_b = jnp.broadcast_to(gate, (cs, tt))
            xj = h_ref[pl.ds(off, tt), :].astype(BF16)

            def chunk_body(c, carry2):
                onehot = sel_b & (rank_b == slot0 + (c * cs).astype(F32))
                pm = jnp.where(onehot, 1.0, 0.0).astype(BF16)
                xs = jnp.dot(pm, xj, preferred_element_type=F32).astype(BF16)
                gg = jnp.dot(xs, wg_ref[0], preferred_element_type=F32) + bg_ref[0]
                uu = jnp.dot(xs, wu_ref[0], preferred_element_type=F32) + bu_ref[0]
                gg = jnp.minimum(gg, SWIGLU_LIMIT)
                uu = jnp.clip(uu, -SWIGLU_LIMIT, SWIGLU_LIMIT)
                hh = gg * jax.nn.sigmoid(SWIGLU_ALPHA * gg) * (uu + 1.0)
                y = jnp.dot(hh.astype(BF16), wd_ref[0], preferred_element_type=F32) + bd_ref[0]
                gs = jnp.sum(jnp.where(onehot, gate_b, 0.0), axis=1, keepdims=True)
                o_ref[pl.ds(off, tt), :] += _tn_dot(pm, (y * gs).astype(BF16))
                return carry2

            lax.fori_loop(0, (n + cs - 1) // cs, chunk_body, 0)

        return carry

    lax.fori_loop(0, nsub, sub_body, 0)

    @pl.when(e == n_exp - 1)
    def _():
        o_ref[...] = _layer_norm(DEEPNORM_ALPHA * h_ref[...] + o_ref[...], g_ref[...], b_ref[...])


def _moe(h, comb, wg, bg, wu, bu, wd, bd, g, b, tb, tt, cs):
    n, d = h.shape
    n_exp, _, f = wg.shape
    nsub = tb // tt
    nblk = n // tb
    comb_t = comb.T
    counts = jnp.sum((comb_t > 0.0).reshape(n_exp * nblk * nsub, tt), axis=-1).astype(I32)
    comb_t = comb_t.reshape(n_exp, nblk, nsub, tt)
    upper = (jnp.arange(tt)[:, None] <= jnp.arange(tt)[None, :]).astype(BF16)
    wspec = lambda a, b_: pl.BlockSpec((1, a, b_), lambda i, e, c: (e, 0, 0))
    vec = pl.BlockSpec((1, d), lambda i, e, c: (0, 0))
    return pl.pallas_call(
        functools.partial(_moe_kernel, tt=tt, cs=cs, nsub=nsub),
        grid_spec=pltpu.PrefetchScalarGridSpec(
            num_scalar_prefetch=1,
            grid=(nblk, n_exp),
            in_specs=[pl.BlockSpec((tb, d), lambda i, e, c: (i, 0)),
                      pl.BlockSpec((1, 1, nsub, tt), lambda i, e, c: (e, i, 0, 0)),
                      pl.BlockSpec((tt, tt), lambda i, e, c: (0, 0)),
                      wspec(d, f), wspec(1, f), wspec(d, f), wspec(1, f), wspec(f, d), wspec(1, d), vec, vec],
            out_specs=pl.BlockSpec((tb, d), lambda i, e, c: (i, 0)),
        ),
        out_shape=jax.ShapeDtypeStruct((n, d), F32),
        compiler_params=_cparams(("parallel", "arbitrary")),
        name="moe",
    )(counts, h, comb_t, upper, wg, bg.reshape(n_exp, 1, f), wu, bu.reshape(n_exp, 1, f),
      wd, bd.reshape(n_exp, 1, d), g, b)


def _pad_w_in(w_in):
    d = w_in.shape[0]
    return jnp.concatenate([w_in, jnp.zeros((d, _IN_COLS_PAD - _IN_COLS), w_in.dtype)], axis=1).astype(BF16)


def _prep_params(p, l=0):
    half = p["w_out"].shape[1] // 2
    vec = lambda a: a[l].reshape(1, -1)
    return {
        "wo_a": p["w_out"][l, :half].astype(BF16), "wo_b": p["w_out"][l, half:].astype(BF16),
        "ln1_g": vec(p["ln1_g"]), "ln1_b": vec(p["ln1_b"]), "ln2_g": vec(p["ln2_g"]), "ln2_b": vec(p["ln2_b"]),
        "router_w": p["router_w"][l], "router_b": vec(p["router_b"]),
        "w_gate": p["w_gate"][l].astype(BF16), "b_gate": p["b_gate"][l],
        "w_up": p["w_up"][l].astype(BF16), "b_up": p["b_up"][l],
        "w_down": p["w_down"][l].astype(BF16), "b_down": p["b_down"][l],
    }


def _pick_tile(n, pref):
    t = min(n, pref)
    while n % t:
        t //= 2
    return t


def _prompt_mixers(x_prompt, w_pad, lam, subln_g, lam_init):
    b, t, d = x_prompt.shape
    n = b * t
    cos, sin = _rope_tables(jnp.arange(t))
    (qa, kaf, kab, vaf, vab, qb, kbf, kbb, vbf, vbb, qi, kif, kib, wi) = _project(
        x_prompt.reshape(n, d), w_pad, cos, sin, _pick_tile(t, 512))
    r3 = lambda a: a.reshape(b, t, a.shape[-1])
    mix_a = _diff_attention(lam, r3(qa), r3(kab), r3(vab), subln_g, lam_init, _pick_tile(t, 512))
    mix_b = _sparse_attention(r3(qi), r3(wi), r3(kib), r3(qb), r3(kbb), r3(vbb), _pick_tile(t, 128),
                              _pick_tile(t, 512))
    rows = (kaf.reshape(1, b, t, A_HEADS, 2 * A_HEAD_DIM), vaf.reshape(1, b, t, A_HEADS, 2 * A_HEAD_DIM),
            kbf.reshape(1, b, t, B_KV_HEADS, B_HEAD_DIM), vbf.reshape(1, b, t, B_KV_HEADS, B_HEAD_DIM),
            kif.reshape(1, b, t, IDX_DIM))
    return mix_a.reshape(n, -1), mix_b.reshape(n, -1), rows


def _decode_mixers(x_sample, w_pad, lam, subln_g, lam_init, caches, page_table, pg):
    ck_diff, cv_diff, ck_sparse, cv_sparse, ck_index = caches
    db, _, d = x_sample.shape
    n_pool, page = ck_diff.shape[:2]
    npg = page_table.shape[1]
    past = npg * page
    cos, sin = _rope_tables(jnp.full((db,), past))
    (qa, kaf, _, vaf, _, qb, kbf, _, vbf, _, qi, kif, _, wi) = _project(x_sample.reshape(db, d), w_pad, cos, sin, db)

    qa3 = qa.reshape(db, A_HEADS, LANES)
    lane = _lane_iota(qa3.shape[1:])[None]
    zero = jnp.zeros_like(qa3)
    qm = jnp.concatenate([jnp.where(lane < HALF, qa3, zero), jnp.where(lane >= HALF, qa3, zero)], axis=1)
    twice = lambda a: jnp.concatenate([a.reshape(db, A_HEADS, LANES)] * 2, axis=1)
    out_a, scores, sc_new = _dec_diff_attention(
        page_table, lam, qm, twice(kaf), twice(vaf), subln_g,
        qi.reshape(db, IDX_HEADS, IDX_DIM), wi.reshape(db, IDX_HEADS, 1), kif.reshape(db, 1, IDX_DIM),
        ck_diff.reshape(n_pool, page * A_HEADS, LANES), cv_diff.reshape(n_pool, page * A_HEADS, LANES),
        jnp.transpose(ck_index, (0, 2, 1)), lam_init, pg)
    mix_a = out_a[:, :A_HEADS].reshape(db, A_HEADS * LANES).astype(BF16)
    sc_all = jnp.concatenate([scores[:, 0], sc_new[:, 0, :1], jnp.zeros((db, LANES - 1), F32)], axis=1)
    keys_t, tau = _dec_select(sc_all.T, past + 1, min(TOPK_MAX, (past + 1) // 4))
    keys = keys_t.T
    tau = jnp.broadcast_to(tau[0].reshape(db, 1, 1), (db, 1, LANES))

    grp = B_HEADS // B_KV_HEADS
    qb4 = qb.reshape(db, B_KV_HEADS, grp, B_HEAD_DIM)
    qs = jnp.concatenate([qb4, jnp.zeros((db, B_KV_HEADS, 8 - grp, B_HEAD_DIM), qb.dtype)], axis=2)
    out_b = _dec_sparse_attention(page_table, qs, keys.reshape(db, 1, -1), tau,
                                  kbf.reshape(db, B_KV_HEADS, 1, B_HEAD_DIM), vbf.reshape(db, B_KV_HEADS, 1, B_HEAD_DIM),
                                  jnp.transpose(ck_sparse, (0, 2, 3, 1)), jnp.transpose(cv_sparse, (0, 2, 3, 1)), pg)
    mix_b = out_b[:, :, :grp].reshape(db, B_HEADS * B_HEAD_DIM).astype(BF16)
    rows = (kaf.reshape(1, db, 1, A_HEADS, 2 * A_HEAD_DIM), vaf.reshape(1, db, 1, A_HEADS, 2 * A_HEAD_DIM),
            kbf.reshape(1, db, 1, B_KV_HEADS, B_HEAD_DIM), vbf.reshape(1, db, 1, B_KV_HEADS, B_HEAD_DIM),
            kif.reshape(1, db, 1, IDX_DIM))
    return mix_a, mix_b, rows


def _slot_rows(tt):
    pack = 16
    want = tt * TOP_K // N_EXPERTS
    want += want // 4
    return max(pack, -(-want // pack) * pack)


def _post_layer(x, mix_a, mix_b, p, tb, tt):
    n, d = x.shape
    h, comb = _post_mixer(x, mix_a, mix_b, p["wo_a"], p["wo_b"], p["ln1_g"], p["ln1_b"], p["router_w"],
                          p["router_b"], _pick_tile(n, 512))
    return _moe(h, comb, p["w_gate"], p["b_gate"], p["w_up"], p["b_up"], p["w_down"], p["b_down"],
                p["ln2_g"], p["ln2_b"], tb, tt, _slot_rows(tt))


def kernel(x_prompt, x_sample, cache_k_diff, cache_v_diff, cache_k_sparse, cache_v_sparse, cache_k_index, page_table,
           w_in, lambda_q1, lambda_k1, lambda_q2, lambda_k2, subln_g, w_out, ln1_g, ln1_b, router_w, router_b,
           w_gate, b_gate, w_up, b_up, w_down, b_down, ln2_g, ln2_b):
    assert w_in.shape[0] == DEPTH
    l = 0
    b, t, d = x_prompt.shape
    db = x_sample.shape[0]
    lam_init = 0.8 - 0.6 * math.exp(-0.3 * l)
    lam = (jnp.exp(jnp.sum(lambda_q1[l] * lambda_k1[l])) - jnp.exp(jnp.sum(lambda_q2[l] * lambda_k2[l]))
           + lam_init).reshape(1, 1)
    g = subln_g[l].reshape(1, -1)
    w_pad = _pad_w_in(w_in[l])
    params = _prep_params(dict(w_out=w_out, ln1_g=ln1_g, ln1_b=ln1_b, router_w=router_w, router_b=router_b,
                               w_gate=w_gate, b_gate=b_gate, w_up=w_up, b_up=b_up, w_down=w_down, b_down=b_down,
                               ln2_g=ln2_g, ln2_b=ln2_b), l)

    mix_a, mix_b, rows_p = _prompt_mixers(x_prompt, w_pad, lam, g, lam_init)
    n = b * t
    y_p = _post_layer(x_prompt.reshape(n, d), mix_a, mix_b, params, _pick_tile(n, 2048), _pick_tile(n, 512))

    caches = (cache_k_diff[l], cache_v_diff[l], cache_k_sparse[l], cache_v_sparse[l], cache_k_index[l])
    mix_a, mix_b, rows_s = _decode_mixers(x_sample, w_pad, lam, g, lam_init, caches, page_table,
                                          _pick_tile(page_table.shape[1], 16))
    y_s = _post_layer(x_sample.reshape(db, d), mix_a, mix_b, params, db, db)
    return (y_p.reshape(b, t, d), y_s.reshape(db, 1, d)) + rows_p + rows_s
```

```python
import functools
import math

import jax
import jax.numpy as jnp
from jax import lax
from jax.experimental import pallas as pl
from jax.experimental.pallas import tpu as pltpu

F32 = jnp.float32
BF16 = jnp.bfloat16
I32 = jnp.int32

A_HEADS = 4
A_HEAD_DIM = 64
B_HEADS = 8
B_KV_HEADS = 2
B_HEAD_DIM = 64
IDX_HEADS = 8
IDX_DIM = 64
TOPK_MAX = 256
N_EXPERTS = 32
TOP_K = 4
SWIGLU_LIMIT = 7.0
SWIGLU_ALPHA = 1.702
ROPE_THETA = 10000.0
LN_EPS = 1e-5
RMS_EPS = 1e-5
DEPTH = 1
DEEPNORM_ALPHA = (2.0 * DEPTH) ** 0.25
IDX_SCALE = (IDX_HEADS * IDX_DIM) ** -0.5

LANES = 128
HALF = 64
NEG_BIG = -1e30
INT_MIN = -(2 ** 31)
VMEM_LIMIT = 56 * 1024 * 1024

_C_QA, _C_KA, _C_VA, _C_QB, _C_KB, _C_VB, _C_QI, _C_KIWI = 0, 512, 1024, 1536, 2048, 2176, 2304, 2816
_IN_COLS = 2888
_IN_COLS_PAD = 2944


def _cparams(sem):
    return pltpu.CompilerParams(dimension_semantics=sem, vmem_limit_bytes=VMEM_LIMIT)


def _nt_dot(a, b):
    return lax.dot_general(a, b, (((1,), (1,)), ((), ())), preferred_element_type=F32)


def _tn_dot(a, b):
    return lax.dot_general(a, b, (((0,), (0,)), ((), ())), preferred_element_type=F32)


def _lane_iota(shape):
    return lax.broadcasted_iota(I32, shape, 1)


def _proj_kernel(x_ref, w_ref, cos_ref, sin_ref,
                 qa_ref, kaf_ref, kab_ref, vaf_ref, vab_ref, qb_ref, kbf_ref, kbb_ref, vbf_ref, vbb_ref,
                 qi_ref, kif_ref, kib_ref, wi_ref):
    xb = x_ref[...].astype(BF16)
    cos = cos_ref[...]
    sin = sin_ref[...]
    lane = _lane_iota(cos.shape)
    first = (lane % HALF) < (HALF // 2)

    def seg(c0, n):
        return jnp.dot(xb, w_ref[:, c0:c0 + n], preferred_element_type=F32)

    def rope(h):
        rot = jnp.where(first, pltpu.roll(h, LANES - HALF // 2, 1), pltpu.roll(h, HALF // 2, 1))
        return h * cos + rot * sin

    def blocks(h):
        return [h[:, k * LANES:(k + 1) * LANES] for k in range(h.shape[1] // LANES)]

    scale = A_HEAD_DIM ** -0.5 * math.log2(math.e)
    for k, h in enumerate(blocks(seg(_C_QA, 512))):
        qa_ref[:, k * LANES:(k + 1) * LANES] = (rope(h) * scale).astype(BF16)
    for k, h in enumerate(blocks(seg(_C_KA, 512))):
        r = rope(h)
        kaf_ref[:, k * LANES:(k + 1) * LANES] = r
        kab_ref[:, k * LANES:(k + 1) * LANES] = r.astype(BF16)
    va = seg(_C_VA, 512)
    vaf_ref[...] = va
    vab_ref[...] = va.astype(BF16)
    for k, h in enumerate(blocks(seg(_C_QB, 512))):
        qb_ref[:, k * LANES:(k + 1) * LANES] = (rope(h) * scale).astype(BF16)
    kbvb = seg(_C_KB, 256)
    kb = rope(kbvb[:, :LANES])
    kbf_ref[...] = kb
    kbb_ref[...] = kb.astype(BF16)
    vb = kbvb[:, LANES:]
    vbf_ref[...] = vb
    vbb_ref[...] = vb.astype(BF16)
    for k, h in enumerate(blocks(seg(_C_QI, 512))):
        qi_ref[:, k * LANES:(k + 1) * LANES] = rope(h).astype(BF16)
    kiwi = seg(_C_KIWI, LANES)
    ki = rope(kiwi)
    kif_ref[...] = ki[:, :IDX_DIM]
    kib_ref[...] = jnp.where(lane < HALF, ki, pltpu.roll(ki, HALF, 1)).astype(BF16)
    wi_ref[...] = pltpu.roll(kiwi, HALF, 1)[:, :IDX_HEADS]


def _project(x, w_pad, cos, sin, tm):
    n, d = x.shape
    t_tab = cos.shape[0]
    n_tab = t_tab // tm
    row = lambda i: (i, 0)
    shapes = [((n, 512), BF16), ((n, 512), F32), ((n, 512), BF16), ((n, 512), F32), ((n, 512), BF16),
              ((n, 512), BF16), ((n, LANES), F32), ((n, LANES), BF16), ((n, LANES), F32), ((n, LANES), BF16),
              ((n, 512), BF16), ((n, IDX_DIM), F32), ((n, LANES), BF16), ((n, IDX_HEADS), F32)]
    return pl.pallas_call(
        _proj_kernel,
        grid=(n // tm,),
        in_specs=[pl.BlockSpec((tm, d), row),
                  pl.BlockSpec((d, _IN_COLS_PAD), lambda i: (0, 0)),
                  pl.BlockSpec((tm, LANES), lambda i: (i % n_tab, 0)),
                  pl.BlockSpec((tm, LANES), lambda i: (i % n_tab, 0))],
        out_specs=[pl.BlockSpec((tm, s[1]), row) for s, _ in shapes],
        out_shape=[jax.ShapeDtypeStruct(s, dt) for s, dt in shapes],
        compiler_params=_cparams(("parallel",)),
        name="proj",
    )(x, w_pad, cos, sin)


def _rope_tables(pos):
    inv = ROPE_THETA ** (-jnp.arange(0, HALF, 2, dtype=F32) / HALF)
    ang = pos.astype(F32)[:, None] * inv[None, :]
    cos = jnp.concatenate([jnp.cos(ang)] * 4, axis=-1)
    sin = jnp.concatenate([-jnp.sin(ang), jnp.sin(ang)] * 2, axis=-1)
    return cos, sin


def _diff_finish(acc, l, lam, g, lam_init, rows):
    o = acc[:rows] / l[:rows] - lam * (acc[rows:] / l[rows:])
    o = o * lax.rsqrt(jnp.mean(o * o, axis=-1, keepdims=True) + RMS_EPS)
    return o * g * (1.0 - lam_init)


ONES_ROWS = 16


def _with_ones_rows(vt):
    return jnp.concatenate([vt, jnp.ones(vt.shape[:-2] + (ONES_ROWS, vt.shape[-1]), vt.dtype)], axis=-2)


def _diff_kernel(lam_ref, q_ref, k_ref, vt_ref, g_ref, o_ref, m_scr, acc_scr, *, tq, lam_init):
    qi = pl.program_id(2)
    q = q_ref[0]
    lane = _lane_iota(q.shape)
    zero = jnp.zeros_like(q)
    qq = jnp.concatenate([jnp.where(lane < HALF, q, zero), jnp.where(lane >= HALF, q, zero)], axis=0)
    m_scr[...] = jnp.full(m_scr.shape, NEG_BIG, F32)
    acc_scr[...] = jnp.zeros(acc_scr.shape, F32)

    def step(j, diagonal):
        off = pl.multiple_of(j * tq, tq)
        s = _nt_dot(k_ref[0, pl.ds(off, tq), :], qq)
        if diagonal:
            kpos = lax.broadcasted_iota(I32, s.shape, 0)
            qpos = lax.broadcasted_iota(I32, s.shape, 1) % tq
            s = jnp.where(kpos <= qpos, s, NEG_BIG)
        m_old = m_scr[...]
        m_new = jnp.maximum(m_old, jnp.max(s, axis=0, keepdims=True))
        a = jnp.exp2(m_old - m_new)
        p = jnp.exp2(s - m_new)
        acc_scr[...] = a * acc_scr[...] + jnp.dot(vt_ref[0, 0, j], p.astype(BF16), preferred_element_type=F32)
        m_scr[...] = m_new

    def body(j, carry):
        step(j, False)
        return carry

    lax.fori_loop(0, qi, body, 0)
    step(qi, True)
    o = acc_scr[:LANES, :] * (1.0 / acc_scr[LANES:LANES + 1, :])
    o = o[:, :tq] - lam_ref[0, 0] * o[:, tq:]
    o = o * lax.rsqrt(jnp.mean(o * o, axis=0, keepdims=True) + RMS_EPS)
    o = o * g_ref[...] * (1.0 - lam_init)
    o_ref[0] = o.T.astype(o_ref.dtype)


def _diff_attention(lam, qa, ka, va, g, lam_init, tq):
    b, t, _ = qa.shape
    nblk = t // tq
    vt = _with_ones_rows(jnp.transpose(va.reshape(b, nblk, tq, A_HEADS, LANES), (0, 3, 1, 4, 2)))
    rows = LANES + ONES_ROWS
    return pl.pallas_call(
        functools.partial(_diff_kernel, tq=tq, lam_init=lam_init),
        grid=(b, A_HEADS, nblk),
        in_specs=[pl.BlockSpec(memory_space=pltpu.SMEM),
                  pl.BlockSpec((1, tq, LANES), lambda b_, h, i: (b_, i, h)),
                  pl.BlockSpec((1, t, LANES), lambda b_, h, i: (b_, 0, h)),
                  pl.BlockSpec((1, 1, nblk, rows, tq), lambda b_, h, i: (b_, h, 0, 0, 0)),
                  pl.BlockSpec((LANES, 1), lambda b_, h, i: (0, 0))],
        out_specs=pl.BlockSpec((1, tq, LANES), lambda b_, h, i: (b_, i, h)),
        out_shape=jax.ShapeDtypeStruct((b, t, A_HEADS * LANES), BF16),
        scratch_shapes=[pltpu.VMEM((1, 2 * tq), F32), pltpu.VMEM((rows, 2 * tq), F32)],
        compiler_params=_cparams(("parallel", "parallel", "arbitrary")),
        name="diff_attn",
    )(lam, qa, ka, vt, g.reshape(LANES, 1))


def _float_key(x):
    b = lax.bitcast_convert_type(x + 0.0, I32)
    return b ^ ((b >> 31) & 0x7FFFFFFF)


def _select_topk(key_scr, nch, ck, topk):
    nq = key_scr.shape[1]
    kf = float(topk)

    grp = 64

    def counts(tau_b, strict):
        def body(c, cnt):
            off = pl.multiple_of(c * ck, ck)
            for g in range(ck // grp):
                kk = key_scr[pl.ds(off + g * grp, grp), :]
                hit = (kk > tau_b) if strict else (kk >= tau_b)
                cnt = cnt + jnp.where(hit, 1, 0)
            return cnt
        cnt = lax.fori_loop(0, nch, body, jnp.zeros((grp, nq), I32))
        return jnp.sum(cnt.astype(F32), axis=0, keepdims=True)

    def bcast(v, rows=grp):
        return jnp.broadcast_to(v, (rows, nq))

    cnt0 = counts(jnp.zeros((grp, nq), I32), False)
    tau = jnp.where(cnt0 >= kf, 0, INT_MIN).astype(I32)

    def bit_body(i, tau):
        cand = tau | jnp.left_shift(jnp.int32(1), 30 - i)
        cnt = counts(bcast(cand), False)
        return jnp.where(cnt >= kf, cand, tau)

    tau = lax.fori_loop(0, 31, bit_body, tau)
    cnt_ge = counts(bcast(tau), False)
    cnt_gt = counts(bcast(tau), True)
    overfull = jnp.where((cnt_ge > kf) & (tau > INT_MIN), 1.0, 0.0)

    @pl.when(jnp.max(overfull) > 0.0)
    def _():
        tau_b = bcast(tau, LANES)
        need_b = bcast(kf - cnt_gt, LANES)
        lower = jnp.where(lax.broadcasted_iota(I32, (LANES, LANES), 0)
                          >= lax.broadcasted_iota(I32, (LANES, LANES), 1), 1.0, 0.0).astype(BF16)

        def body(c, run):
            off = pl.multiple_of(c * ck, ck)
            for g in range(ck // LANES):
                sl = pl.ds(off + g * LANES, LANES)
                kk = key_scr[sl, :]
                eq = kk == tau_b
                eqf = jnp.where(eq, 1.0, 0.0)
                pre = jnp.dot(lower, eqf.astype(BF16), preferred_element_type=F32)
                drop = eq & ((pre + run) > need_b)
                key_scr[sl, :] = jnp.where(drop, INT_MIN, kk)
                run = run + jnp.sum(eqf, axis=0, keepdims=True)
            return run

        lax.fori_loop(0, nch, body, jnp.zeros((1, nq), F32))

    return tau


def _stack_heads(q, n_heads):
    lane = _lane_iota((q.shape[0], LANES))
    zero = jnp.zeros((q.shape[0], LANES), q.dtype)
    parts = []
    for h in range(n_heads):
        blk = q[:, (h // 2) * LANES:(h // 2 + 1) * LANES]
        parts.append(jnp.where((lane < HALF) == (h % 2 == 0), blk, zero))
    return jnp.concatenate(parts, axis=0)


M_INIT = -1e29


def _sparse_kernel(qi_ref, wt_ref, ki_ref, qb_ref, kb_ref, vt_ref, o_ref, key_scr, m_scr, acc_scr,
                   *, tq, ck, topk):
    i = pl.program_id(1)
    nch = ((i + 1) * tq + ck - 1) // ck
    kpos0 = lax.broadcasted_iota(I32, (ck, tq), 0)
    qpos = i * tq + lax.broadcasted_iota(I32, (ck, tq), 1)

    qs = _stack_heads(qi_ref[0], IDX_HEADS)
    wt = wt_ref[0]

    def score_body(c, carry):
        off = pl.multiple_of(c * ck, ck)
        d = _nt_dot(ki_ref[0, pl.ds(off, ck), :], qs)
        sc = jnp.zeros((ck, tq), F32)
        for h in range(IDX_HEADS):
            sc = sc + jnp.maximum(d[:, h * tq:(h + 1) * tq], 0.0) * wt[h:h + 1, :]
        sc = sc * IDX_SCALE
        key_scr[pl.ds(off, ck), :] = jnp.where(kpos0 + off <= qpos, _float_key(sc), INT_MIN)
        return carry

    lax.fori_loop(0, nch, score_body, 0)
    tau = _select_topk(key_scr, nch, ck, topk)
    tau_b = jnp.broadcast_to(tau, (ck, tq))

    qh = []
    qb = qb_ref[0]
    lane = _lane_iota((tq, LANES))
    zero = jnp.zeros((tq, LANES), qb.dtype)
    grp = B_HEADS // B_KV_HEADS
    for h in range(B_HEADS):
        blk = qb[:, (h // 2) * LANES:(h // 2 + 1) * LANES]
        if h % 2 != h // grp:
            blk = pltpu.roll(blk, HALF, 1)
        qh.append(jnp.where((lane >= HALF) == (h // grp == 1), blk, zero))
    qs2 = jnp.concatenate(qh, axis=0)
    m_scr[...] = jnp.full(m_scr.shape, M_INIT, F32)
    acc_scr[...] = jnp.zeros(acc_scr.shape, F32)

    def attn_body(c, carry):
        off = pl.multiple_of(c * ck, ck)
        sel = (key_scr[pl.ds(off, ck), :] >= tau_b) & (kpos0 + off <= qpos)
        bias = jnp.where(sel, 0.0, NEG_BIG)
        s = _nt_dot(kb_ref[0, pl.ds(off, ck), :], qs2)
        ps, scales = [], []
        for h in range(B_HEADS):
            hs = slice(h * tq, (h + 1) * tq)
            sh = s[:, hs] + bias
            m_old = m_scr[:, hs]
            m_new = jnp.maximum(m_old, jnp.max(sh, axis=0, keepdims=True))
            a = jnp.exp2(m_old - m_new)
            p = jnp.exp2(sh - m_new)
            m_scr[:, hs] = m_new
            ps.append(p.astype(BF16))
            scales.append(a)
        pv = jnp.dot(vt_ref[0, c], jnp.concatenate(ps, axis=1), preferred_element_type=F32)
        acc_scr[...] = jnp.concatenate(scales, axis=1) * acc_scr[...] + pv
        return carry

    lax.fori_loop(0, nch, attn_body, 0)
    o = acc_scr[:LANES, :] * (1.0 / acc_scr[LANES:LANES + 1, :])
    parts = [o[(h // grp) * HALF:(h // grp + 1) * HALF, h * tq:(h + 1) * tq] for h in range(B_HEADS)]
    o_ref[0] = jnp.concatenate(parts, axis=0).T.astype(o_ref.dtype)


def _sparse_attention(qi, wi, ki2, qb, kb, vb, tq, ck):
    b, t, _ = qi.shape
    topk = min(TOPK_MAX, t // 4)
    wt = jnp.transpose(wi, (0, 2, 1))
    vt = _with_ones_rows(jnp.transpose(vb.reshape(b, t // ck, ck, LANES), (0, 1, 3, 2)))
    rows = LANES + ONES_ROWS
    blk_q = lambda w: pl.BlockSpec((1, tq, w), lambda b_, i: (b_, i, 0))
    blk_k = pl.BlockSpec((1, t, LANES), lambda b_, i: (b_, 0, 0))
    return pl.pallas_call(
        functools.partial(_sparse_kernel, tq=tq, ck=ck, topk=topk),
        grid=(b, t // tq),
        in_specs=[blk_q(512), pl.BlockSpec((1, IDX_HEADS, tq), lambda b_, i: (b_, 0, i)), blk_k, blk_q(512), blk_k,
                  pl.BlockSpec((1, t // ck, rows, ck), lambda b_, i: (b_, 0, 0, 0))],
        out_specs=blk_q(512),
        out_shape=jax.ShapeDtypeStruct((b, t, 512), BF16),
        scratch_shapes=[pltpu.VMEM((t, tq), I32), pltpu.VMEM((1, B_HEADS * tq), F32),
                        pltpu.VMEM((rows, B_HEADS * tq), F32)],
        compiler_params=_cparams(("parallel", "arbitrary")),
        name="sparse_attn",
    )(qi, wt, ki2, qb, kb, vt)


def _page_specs(n, block, pg):
    zeros = (0,) * (len(block) - 1)
    return [pl.BlockSpec(block, functools.partial(lambda i, b, j, pt: (pt[b, j * pg + i],) + zeros, i))
            for i in range(n)]


def _dec_diff_kernel(pt_ref, lam_ref, q_ref, kn_ref, vn_ref, g_ref, qi_ref, w_ref, kin_ref, *rest, pg, lam_init):
    k_refs, v_refs, ki_refs = rest[:pg], rest[pg:2 * pg], rest[2 * pg:3 * pg]
    o_ref, sc_ref, new_ref, m_scr, l_scr, acc_scr, kbuf, vbuf, kibuf = rest[3 * pg:]
    j = pl.program_id(1)

    @pl.when(j == 0)
    def _():
        m_scr[...] = jnp.full(m_scr.shape, NEG_BIG, F32)
        l_scr[...] = jnp.zeros(l_scr.shape, F32)
        acc_scr[...] = jnp.zeros(acc_scr.shape, F32)

    rows_pg = k_refs[0].shape[1]
    page = ki_refs[0].shape[2]
    for i in range(pg):
        kbuf[i * rows_pg:(i + 1) * rows_pg, :] = k_refs[i][0].astype(BF16)
        vbuf[i * rows_pg:(i + 1) * rows_pg, :] = v_refs[i][0].astype(BF16)
        kibuf[:, i * page:(i + 1) * page] = ki_refs[i][0].astype(BF16)

    q = q_ref[0]
    s = _nt_dot(q, kbuf[...])
    same_head = (_lane_iota(s.shape) % A_HEADS) == (lax.broadcasted_iota(I32, s.shape, 0) % A_HEADS)
    s = jnp.where(same_head, s, NEG_BIG)
    m_old = m_scr[...]
    m_new = jnp.maximum(m_old, jnp.max(s, axis=1, keepdims=True))
    a = jnp.exp2(m_old - m_new)
    p = jnp.exp2(s - m_new).astype(BF16)
    l_scr[...] = a * l_scr[...] + jnp.sum(p.astype(F32), axis=1, keepdims=True)
    acc_scr[...] = a * acc_scr[...] + jnp.dot(p, vbuf[...], preferred_element_type=F32)
    m_scr[...] = m_new

    qi = qi_ref[0]
    w = w_ref[0]
    d = jnp.dot(qi, kibuf[...], preferred_element_type=F32)
    sc_ref[0] = jnp.sum(jnp.maximum(d, 0.0) * w, axis=0, keepdims=True) * IDX_SCALE

    @pl.when(j == pl.num_programs(1) - 1)
    def _():
        s_new = jnp.sum(q.astype(F32) * kn_ref[0], axis=1, keepdims=True)
        m_old = m_scr[...]
        m_new = jnp.maximum(m_old, s_new)
        a = jnp.exp2(m_old - m_new)
        pn = jnp.exp2(s_new - m_new)
        l = a * l_scr[...] + pn
        acc = a * acc_scr[...] + pn * vn_ref[0]
        o = _diff_finish(acc, l, lam_ref[0, 0], g_ref[...], lam_init, A_HEADS)
        o_ref[0] = jnp.concatenate([o, jnp.zeros_like(o)], axis=0)
        dn = jnp.sum(qi.astype(F32) * kin_ref[0], axis=1, keepdims=True)
        scn = jnp.sum(jnp.maximum(dn, 0.0) * w, axis=0, keepdims=True) * IDX_SCALE
        new_ref[0] = jnp.broadcast_to(scn, new_ref.shape[1:])


def _dec_diff_attention(pt, lam, qm, knew, vnew, g, qi, wi, kinew, kpages, vpages, kipages, lam_init, pg):
    db, npg = pt.shape
    rows_pg = kpages.shape[1]
    page = kipages.shape[2]
    per_b = lambda shape: pl.BlockSpec(shape, lambda b, j, pt_: (b,) + (0,) * (len(shape) - 1))
    return pl.pallas_call(
        functools.partial(_dec_diff_kernel, pg=pg, lam_init=lam_init),
        grid_spec=pltpu.PrefetchScalarGridSpec(
            num_scalar_prefetch=1,
            grid=(db, npg // pg),
            in_specs=[pl.BlockSpec(memory_space=pltpu.SMEM), per_b((1, 8, LANES)), per_b((1, 8, LANES)),
                      per_b((1, 8, LANES)), pl.BlockSpec((1, LANES), lambda b, j, pt_: (0, 0)),
                      per_b((1, IDX_HEADS, IDX_DIM)), per_b((1, IDX_HEADS, 1)), per_b((1, 1, IDX_DIM))]
                     + _page_specs(pg, (1,) + kpages.shape[1:], pg) + _page_specs(pg, (1,) + vpages.shape[1:], pg)
                     + _page_specs(pg, (1,) + kipages.shape[1:], pg),
            out_specs=[per_b((1, 8, LANES)), pl.BlockSpec((1, 1, pg * page), lambda b, j, pt_: (b, 0, j)),
                       per_b((1, 1, LANES))],
            scratch_shapes=[pltpu.VMEM((8, 1), F32), pltpu.VMEM((8, 1), F32), pltpu.VMEM((8, LANES), F32),
                            pltpu.VMEM((pg * rows_pg, LANES), BF16), pltpu.VMEM((pg * rows_pg, LANES), BF16),
                            pltpu.VMEM((IDX_DIM, pg * page), BF16)],
        ),
        out_shape=[jax.ShapeDtypeStruct((db, 8, LANES), F32), jax.ShapeDtypeStruct((db, 1, npg * page), F32),
                   jax.ShapeDtypeStruct((db, 1, LANES), F32)],
        compiler_params=_cparams(("parallel", "arbitrary")),
        name="dec_diff_attn",
    )(pt, lam, qm, knew, vnew, g, qi, wi, kinew, *([kpages] * pg), *([vpages] * pg), *([kipages] * pg))


def _dec_select_kernel(sc_ref, key_ref, tau_ref, *, n_keys, topk):
    width, nq = sc_ref.shape
    kpos = lax.broadcasted_iota(I32, (width, nq), 0)
    key_ref[...] = jnp.where(kpos < n_keys, _float_key(sc_ref[...]), INT_MIN)
    tau = _select_topk(key_ref, width // LANES, LANES, topk)
    tau_ref[...] = jnp.broadcast_to(tau, tau_ref.shape)


def _dec_select(scores_t, n_keys, topk):
    width, nq = scores_t.shape
    return pl.pallas_call(
        functools.partial(_dec_select_kernel, n_keys=n_keys, topk=topk),
        out_shape=[jax.ShapeDtypeStruct((width, nq), I32), jax.ShapeDtypeStruct((8, nq), I32)],
        compiler_params=pltpu.CompilerParams(vmem_limit_bytes=VMEM_LIMIT),
        name="dec_select",
    )(scores_t)


def _dec_sparse_kernel(pt_ref, q_ref, key_ref, keyn_ref, tau_ref, kn_ref, vn_ref, *rest, pg):
    k_refs, v_refs = rest[:pg], rest[pg:2 * pg]
    o_ref, m_scr, l_scr, acc_scr, kbuf, vbuf = rest[2 * pg:]
    j = pl.program_id(1)

    @pl.when(j == 0)
    def _():
        m_scr[...] = jnp.full(m_scr.shape, NEG_BIG, F32)
        l_scr[...] = jnp.zeros(l_scr.shape, F32)
        acc_scr[...] = jnp.zeros(acc_scr.shape, F32)

    page = k_refs[0].shape[3]
    for i in range(pg):
        kbuf[:, :, i * page:(i + 1) * page] = k_refs[i][0].astype(BF16)
        vbuf[:, :, i * page:(i + 1) * page] = v_refs[i][0].astype(BF16)
    tau = tau_ref[0][:, 0:1]
    sel = jnp.broadcast_to(key_ref[0] >= tau, (8, pg * page))
    for g in range(B_KV_HEADS):
        q = q_ref[0, g]
        s = jnp.dot(q, kbuf[g], preferred_element_type=F32)
        s = jnp.where(sel, s, NEG_BIG)
        m_old = m_scr[g]
        m_new = jnp.maximum(m_old, jnp.max(s, axis=1, keepdims=True))
        a = jnp.exp2(m_old - m_new)
        p = jnp.where(sel, jnp.exp2(s - m_new), 0.0).astype(BF16)
        l_scr[g] = a * l_scr[g] + jnp.sum(p.astype(F32), axis=1, keepdims=True)
        acc_scr[g] = a * acc_scr[g] + _nt_dot(p, vbuf[g])
        m_scr[g] = m_new

    @pl.when(j == pl.num_programs(1) - 1)
    def _():
        sel_new = keyn_ref[0][:, 0:1] >= tau
        for g in range(B_KV_HEADS):
            s_new = jnp.sum(q_ref[0, g].astype(F32) * kn_ref[0, g], axis=1, keepdims=True)
            s_new = jnp.where(sel_new, s_new, NEG_BIG)
            m_old = m_scr[g]
            m_new = jnp.maximum(m_old, s_new)
            a = jnp.exp2(m_old - m_new)
            pn = jnp.where(sel_new, jnp.exp2(s_new - m_new), 0.0)
            l = a * l_scr[g] + pn
            o_ref[0, g] = (a * acc_scr[g] + pn * vn_ref[0, g]) / l


def _dec_sparse_attention(pt, q, keys, tau, kbnew, vbnew, kpages, vpages, pg):
    db, npg = pt.shape
    page = kpages.shape[3]
    per_b = lambda shape: pl.BlockSpec(shape, lambda b, j, pt_: (b,) + (0,) * (len(shape) - 1))
    blk = (1,) + kpages.shape[1:]
    return pl.pallas_call(
        functools.partial(_dec_sparse_kernel, pg=pg),
        grid_spec=pltpu.PrefetchScalarGridSpec(
            num_scalar_prefetch=1,
            grid=(db, npg // pg),
            in_specs=[per_b((1, B_KV_HEADS, 8, B_HEAD_DIM)),
                      pl.BlockSpec((1, 1, pg * page), lambda b, j, pt_: (b, 0, j)),
                      pl.BlockSpec((1, 1, LANES), lambda b, j, pt_: (b, 0, npg * page // LANES)),
                      per_b((1, 1, LANES)), per_b((1, B_KV_HEADS, 1, B_HEAD_DIM)),
                      per_b((1, B_KV_HEADS, 1, B_HEAD_DIM))]
                     + _page_specs(pg, blk, pg) + _page_specs(pg, blk, pg),
            out_specs=per_b((1, B_KV_HEADS, 8, B_HEAD_DIM)),
            scratch_shapes=[pltpu.VMEM((B_KV_HEADS, 8, 1), F32), pltpu.VMEM((B_KV_HEADS, 8, 1), F32),
                            pltpu.VMEM((B_KV_HEADS, 8, B_HEAD_DIM), F32),
                            pltpu.VMEM((B_KV_HEADS, B_HEAD_DIM, pg * page), BF16),
                            pltpu.VMEM((B_KV_HEADS, B_HEAD_DIM, pg * page), BF16)],
        ),
        out_shape=jax.ShapeDtypeStruct((db, B_KV_HEADS, 8, B_HEAD_DIM), F32),
        compiler_params=_cparams(("parallel", "arbitrary")),
        name="dec_sparse_attn",
    )(pt, q, keys, keys, tau, kbnew, vbnew, *([kpages] * pg), *([vpages] * pg))


def _layer_norm(x, g, b):
    mu = jnp.mean(x, axis=-1, keepdims=True)
    xc = x - mu
    var = jnp.mean(xc * xc, axis=-1, keepdims=True)
    return xc * lax.rsqrt(var + LN_EPS) * g + b


def _post_kernel(x_ref, ma_ref, mb_ref, woa_ref, wob_ref, g_ref, b_ref, rw_ref, rb_ref, h_ref, c_ref):
    y = (DEEPNORM_ALPHA * x_ref[...]
         + jnp.dot(ma_ref[...], woa_ref[...], preferred_element_type=F32)
         + jnp.dot(mb_ref[...], wob_ref[...], preferred_element_type=F32))
    h = _layer_norm(y, g_ref[...], b_ref[...])
    h_ref[...] = h
    logits = jnp.dot(h, rw_ref[...], preferred_element_type=F32, precision=lax.Precision.HIGHEST) + rb_ref[...]
    lane = _lane_iota(logits.shape)
    work = logits
    sel = jnp.zeros(logits.shape, jnp.bool_)
    for _ in range(TOP_K):
        mx = jnp.max(work, axis=-1, keepdims=True)
        idx = jnp.min(jnp.where(work == mx, lane, N_EXPERTS), axis=-1, keepdims=True)
        pick = lane == idx
        sel = sel | pick
        work = jnp.where(pick, -jnp.inf, work)
    e = jnp.where(sel, jnp.exp(logits - jnp.max(logits, axis=-1, keepdims=True)), 0.0)
    c_ref[...] = e / jnp.sum(e, axis=-1, keepdims=True)


def _post_mixer(x, mix_a, mix_b, wo_a, wo_b, g, b, rw, rb, tm):
    n, d = x.shape
    row = lambda w: pl.BlockSpec((tm, w), lambda i: (i, 0))
    full = lambda a: pl.BlockSpec(a.shape, lambda i: (0, 0))
    return pl.pallas_call(
        _post_kernel,
        grid=(n // tm,),
        in_specs=[row(d), row(mix_a.shape[1]), row(mix_b.shape[1]), full(wo_a), full(wo_b), full(g), full(b),
                  full(rw), full(rb)],
        out_specs=[row(d), row(N_EXPERTS)],
        out_shape=[jax.ShapeDtypeStruct((n, d), F32), jax.ShapeDtypeStruct((n, N_EXPERTS), F32)],
        compiler_params=_cparams(("parallel",)),
        name="post_mixer",
    )(x, mix_a, mix_b, wo_a, wo_b, g, b, rw, rb)


def _moe_kernel(cnt_ref, h_ref, comb_ref, upper_ref, wg_ref, bg_ref, wu_ref, bu_ref, wd_ref, bd_ref,
                g_ref, b_ref, o_ref, xs_scr, pm_scr, gs_scr, ys_scr, *, tt, cs, nsub):
    i = pl.program_id(0)
    e = pl.program_id(1)
    n_exp = pl.num_programs(1)

    @pl.when(e == 0)
    def _():
        o_ref[...] = jnp.zeros(o_ref.shape, F32)

    slot0 = lax.broadcasted_iota(I32, (cs, tt), 0).astype(F32) + 1.0

    def onehot_of(j, first):
        gate = comb_ref[0, 0, pl.ds(j, 1), :]
        sel = gate > 0.0
        self_ = jnp.broadcast_to(jnp.where(sel, 1.0, 0.0), (8, tt)).astype(BF16)
        rank = jnp.dot(self_, upper_ref[...], preferred_element_type=F32)[0:1]
        onehot = jnp.broadcast_to(sel, (cs, tt)) & (jnp.broadcast_to(rank, (cs, tt)) == slot0 + first)
        return onehot, jnp.broadcast_to(gate, (cs, tt))

    def expert(xs):
        gg = jnp.dot(xs, wg_ref[0], preferred_element_type=F32) + bg_ref[0]
        uu = jnp.dot(xs, wu_ref[0], preferred_element_type=F32) + bu_ref[0]
        gg = jnp.minimum(gg, SWIGLU_LIMIT)
        uu = jnp.clip(uu, -SWIGLU_LIMIT, SWIGLU_LIMIT)
        hh = gg * jax.nn.sigmoid(SWIGLU_ALPHA * gg) * (uu + 1.0)
        return jnp.dot(hh.astype(BF16), wd_ref[0], preferred_element_type=F32) + bd_ref[0]

    for j in range(nsub):
        onehot, gate_b = onehot_of(j, 0.0)
        pm = jnp.where(onehot, 1.0, 0.0).astype(BF16)
        pm_scr[j] = pm
        xs_scr[j * cs:(j + 1) * cs, :] = jnp.dot(pm, h_ref[j * tt:(j + 1) * tt, :].astype(BF16),
                                                 preferred_element_type=F32).astype(BF16)
        gs_scr[j * cs:(j + 1) * cs, :] = jnp.sum(jnp.where(onehot, gate_b, 0.0), axis=1, keepdims=True)
    ys_scr[...] = (expert(xs_scr[...]) * gs_scr[...]).astype(BF16)
    for j in range(nsub):
        o_ref[j * tt:(j + 1) * tt, :] += _tn_dot(pm_scr[j], ys_scr[j * cs:(j + 1) * cs, :])

    def sub_body(j, carry):
        n = cnt_ref[(e * pl.num_programs(0) + i) * nsub + j]

        @pl.when(n > cs)
        def _():
            off = pl.multiple_of(j * tt, tt)
            xj = h_ref[pl.ds(off, tt), :].astype(BF16)

            def chunk_body(c, carry2):
                onehot, gate_b = onehot_of(j, (c * cs).astype(F32))
                pm = jnp.where(onehot, 1.0, 0.0).astype(BF16)
                y = expert(jnp.dot(pm, xj, preferred_element_type=F32).astype(BF16))
                gs = jnp.sum(jnp.where(onehot, gate_b, 0.0), axis=1, keepdims=True)
                o_ref[pl.ds(off, tt), :] += _tn_dot(pm, (y * gs).astype(BF16))
                return carry2

            lax.fori_loop(1, (n + cs - 1) // cs, chunk_body, 0)

        return carry

    lax.fori_loop(0, nsub, sub_body, 0)

    @pl.when(e == n_exp - 1)
    def _():
        o_ref[...] = _layer_norm(DEEPNORM_ALPHA * h_ref[...] + o_ref[...], g_ref[...], b_ref[...])


def _moe(h, comb, wg, bg, wu, bu, wd, bd, g, b, tb, tt, cs):
    n, d = h.shape
    n_exp, _, f = wg.shape
    nsub = tb // tt
    nblk = n // tb
    comb_t = comb.T
    counts = jnp.sum((comb_t > 0.0).reshape(n_exp * nblk * nsub, tt), axis=-1).astype(I32)
    comb_t = comb_t.reshape(n_exp, nblk, nsub, tt)
    upper = (jnp.arange(tt)[:, None] <= jnp.arange(tt)[None, :]).astype(BF16)
    wspec = lambda a, b_: pl.BlockSpec((1, a, b_), lambda i, e, c: (e, 0, 0))
    vec = pl.BlockSpec((1, d), lambda i, e, c: (0, 0))
    return pl.pallas_call(
        functools.partial(_moe_kernel, tt=tt, cs=cs, nsub=nsub),
        grid_spec=pltpu.PrefetchScalarGridSpec(
            num_scalar_prefetch=1,
            grid=(nblk, n_exp),
            in_specs=[pl.BlockSpec((tb, d), lambda i, e, c: (i, 0), pipeline_mode=pl.Buffered(1)),
                      pl.BlockSpec((1, 1, nsub, tt), lambda i, e, c: (e, i, 0, 0)),
                      pl.BlockSpec((tt, tt), lambda i, e, c: (0, 0)),
                      wspec(d, f), wspec(1, f), wspec(d, f), wspec(1, f), wspec(f, d), wspec(1, d), vec, vec],
            out_specs=pl.BlockSpec((tb, d), lambda i, e, c: (i, 0)),
            scratch_shapes=[pltpu.VMEM((nsub * cs, d), BF16), pltpu.VMEM((nsub, cs, tt), BF16),
                            pltpu.VMEM((nsub * cs, 1), F32), pltpu.VMEM((nsub * cs, d), BF16)],
        ),
        out_shape=jax.ShapeDtypeStruct((n, d), F32),
        compiler_params=_cparams(("parallel", "arbitrary")),
        name="moe",
    )(counts, h, comb_t, upper, wg, bg.reshape(n_exp, 1, f), wu, bu.reshape(n_exp, 1, f),
      wd, bd.reshape(n_exp, 1, d), g, b)


def _pad_w_in(w_in):
    d = w_in.shape[0]
    return jnp.concatenate([w_in, jnp.zeros((d, _IN_COLS_PAD - _IN_COLS), w_in.dtype)], axis=1).astype(BF16)


def _prep_params(p, l=0):
    half = p["w_out"].shape[1] // 2
    vec = lambda a: a[l].reshape(1, -1)
    return {
        "wo_a": p["w_out"][l, :half].astype(BF16), "wo_b": p["w_out"][l, half:].astype(BF16),
        "ln1_g": vec(p["ln1_g"]), "ln1_b": vec(p["ln1_b"]), "ln2_g": vec(p["ln2_g"]), "ln2_b": vec(p["ln2_b"]),
        "router_w": p["router_w"][l], "router_b": vec(p["router_b"]),
        "w_gate": p["w_gate"][l].astype(BF16), "b_gate": p["b_gate"][l],
        "w_up": p["w_up"][l].astype(BF16), "b_up": p["b_up"][l],
        "w_down": p["w_down"][l].astype(BF16), "b_down": p["b_down"][l],
    }


def _pick_tile(n, pref):
    t = min(n, pref)
    while n % t:
        t //= 2
    return t


def _prompt_mixers(x_prompt, w_pad, lam, subln_g, lam_init):
    b, t, d = x_prompt.shape
    n = b * t
    cos, sin = _rope_tables(jnp.arange(t))
    (qa, kaf, kab, vaf, vab, qb, kbf, kbb, vbf, vbb, qi, kif, kib, wi) = _project(
        x_prompt.reshape(n, d), w_pad, cos, sin, _pick_tile(t, 512))
    r3 = lambda a: a.reshape(b, t, a.shape[-1])
    mix_a = _diff_attention(lam, r3(qa), r3(kab), r3(vab), subln_g, lam_init, _pick_tile(t, 512))
    mix_b = _sparse_attention(r3(qi), r3(wi), r3(kib), r3(qb), r3(kbb), r3(vbb), _pick_tile(t, 128),
                              _pick_tile(t, 512))
    rows = (kaf.reshape(1, b, t, A_HEADS, 2 * A_HEAD_DIM), vaf.reshape(1, b, t, A_HEADS, 2 * A_HEAD_DIM),
            kbf.reshape(1, b, t, B_KV_HEADS, B_HEAD_DIM), vbf.reshape(1, b, t, B_KV_HEADS, B_HEAD_DIM),
            kif.reshape(1, b, t, IDX_DIM))
    return mix_a.reshape(n, -1), mix_b.reshape(n, -1), rows


def _decode_mixers(x_sample, w_pad, lam, subln_g, lam_init, caches, page_table, pg):
    ck_diff, cv_diff, ck_sparse, cv_sparse, ck_index = caches
    db, _, d = x_sample.shape
    n_pool, page = ck_diff.shape[:2]
    npg = page_table.shape[1]
    past = npg * page
    cos, sin = _rope_tables(jnp.full((db,), past))
    (qa, kaf, _, vaf, _, qb, kbf, _, vbf, _, qi, kif, _, wi) = _project(x_sample.reshape(db, d), w_pad, cos, sin, db)

    qa3 = qa.reshape(db, A_HEADS, LANES)
    lane = _lane_iota(qa3.shape[1:])[None]
    zero = jnp.zeros_like(qa3)
    qm = jnp.concatenate([jnp.where(lane < HALF, qa3, zero), jnp.where(lane >= HALF, qa3, zero)], axis=1)
    twice = lambda a: jnp.concatenate([a.reshape(db, A_HEADS, LANES)] * 2, axis=1)
    out_a, scores, sc_new = _dec_diff_attention(
        page_table, lam, qm, twice(kaf), twice(vaf), subln_g,
        qi.reshape(db, IDX_HEADS, IDX_DIM), wi.reshape(db, IDX_HEADS, 1), kif.reshape(db, 1, IDX_DIM),
        ck_diff.reshape(n_pool, page * A_HEADS, LANES), cv_diff.reshape(n_pool, page * A_HEADS, LANES),
        jnp.transpose(ck_index, (0, 2, 1)), lam_init, pg)
    mix_a = out_a[:, :A_HEADS].reshape(db, A_HEADS * LANES).astype(BF16)
    sc_all = jnp.concatenate([scores[:, 0], sc_new[:, 0, :1], jnp.zeros((db, LANES - 1), F32)], axis=1)
    keys_t, tau = _dec_select(sc_all.T, past + 1, min(TOPK_MAX, (past + 1) // 4))
    keys = keys_t.T
    tau = jnp.broadcast_to(tau[0].reshape(db, 1, 1), (db, 1, LANES))

    grp = B_HEADS // B_KV_HEADS
    qb4 = qb.reshape(db, B_KV_HEADS, grp, B_HEAD_DIM)
    qs = jnp.concatenate([qb4, jnp.zeros((db, B_KV_HEADS, 8 - grp, B_HEAD_DIM), qb.dtype)], axis=2)
    out_b = _dec_sparse_attention(page_table, qs, keys.reshape(db, 1, -1), tau,
                                  kbf.reshape(db, B_KV_HEADS, 1, B_HEAD_DIM), vbf.reshape(db, B_KV_HEADS, 1, B_HEAD_DIM),
                                  jnp.transpose(ck_sparse, (0, 2, 3, 1)), jnp.transpose(cv_sparse, (0, 2, 3, 1)),
                                  _pick_tile(npg, 2 * pg))
    mix_b = out_b[:, :, :grp].reshape(db, B_HEADS * B_HEAD_DIM).astype(BF16)
    rows = (kaf.reshape(1, db, 1, A_HEADS, 2 * A_HEAD_DIM), vaf.reshape(1, db, 1, A_HEADS, 2 * A_HEAD_DIM),
            kbf.reshape(1, db, 1, B_KV_HEADS, B_HEAD_DIM), vbf.reshape(1, db, 1, B_KV_HEADS, B_HEAD_DIM),
            kif.reshape(1, db, 1, IDX_DIM))
    return mix_a, mix_b, rows


def _slot_rows(tt):
    pack = 16
    want = tt * TOP_K // N_EXPERTS
    want += want // 2
    return max(pack, -(-want // pack) * pack)


def _post_layer(x, mix_a, mix_b, p, tb, tt):
    n, d = x.shape
    h, comb = _post_mixer(x, mix_a, mix_b, p["wo_a"], p["wo_b"], p["ln1_g"], p["ln1_b"], p["router_w"],
                          p["router_b"], _pick_tile(n, 512))
    return _moe(h, comb, p["w_gate"], p["b_gate"], p["w_up"], p["b_up"], p["w_down"], p["b_down"],
                p["ln2_g"], p["ln2_b"], tb, tt, _slot_rows(tt))


def kernel(x_prompt, x_sample, cache_k_diff, cache_v_diff, cache_k_sparse, cache_v_sparse, cache_k_index, page_table,
           w_in, lambda_q1, lambda_k1, lambda_q2, lambda_k2, subln_g, w_out, ln1_g, ln1_b, router_w, router_b,
           w_gate, b_gate, w_up, b_up, w_down, b_down, ln2_g, ln2_b):
    assert w_in.shape[0] == DEPTH
    l = 0
    b, t, d = x_prompt.shape
    db = x_sample.shape[0]
    lam_init = 0.8 - 0.6 * math.exp(-0.3 * l)
    lam = (jnp.exp(jnp.sum(lambda_q1[l] * lambda_k1[l])) - jnp.exp(jnp.sum(lambda_q2[l] * lambda_k2[l]))
           + lam_init).reshape(1, 1)
    g = subln_g[l].reshape(1, -1)
    w_pad = _pad_w_in(w_in[l])
    params = _prep_params(dict(w_out=w_out, ln1_g=ln1_g, ln1_b=ln1_b, router_w=router_w, router_b=router_b,
                               w_gate=w_gate, b_gate=b_gate, w_up=w_up, b_up=b_up, w_down=w_down, b_down=b_down,
                               ln2_g=ln2_g, ln2_b=ln2_b), l)

    mix_a, mix_b, rows_p = _prompt_mixers(x_prompt, w_pad, lam, g, lam_init)
    n = b * t
    y_p = _post_layer(x_prompt.reshape(n, d), mix_a, mix_b, params, _pick_tile(n, 2048), _pick_tile(n, 512))

    caches = (cache_k_diff[l], cache_v_diff[l], cache_k_sparse[l], cache_v_sparse[l], cache_k_index[l])
    mix_a, mix_b, rows_s = _decode_mixers(x_sample, w_pad, lam, g, lam_init, caches, page_table,
                                          _pick_tile(page_table.shape[1], 16))
    y_s = _post_layer(x_sample.reshape(db, d), mix_a, mix_b, params, db, db)
    return (y_p.reshape(b, t, d), y_s.reshape(db, 1, d)) + rows_p + rows_s
```

```python
import functools
import math

import jax
import jax.numpy as jnp
from jax import lax
from jax.experimental import pallas as pl
from jax.experimental.pallas import tpu as pltpu

F32 = jnp.float32
BF16 = jnp.bfloat16
I32 = jnp.int32

A_HEADS = 4
A_HEAD_DIM = 64
B_HEADS = 8
B_KV_HEADS = 2
B_HEAD_DIM = 64
IDX_HEADS = 8
IDX_DIM = 64
TOPK_MAX = 256
N_EXPERTS = 32
TOP_K = 4
SWIGLU_LIMIT = 7.0
SWIGLU_ALPHA = 1.702
ROPE_THETA = 10000.0
LN_EPS = 1e-5
RMS_EPS = 1e-5
DEPTH = 1
DEEPNORM_ALPHA = (2.0 * DEPTH) ** 0.25
IDX_SCALE = (IDX_HEADS * IDX_DIM) ** -0.5

LANES = 128
HALF = 64
NEG_BIG = -1e30
INT_MIN = -(2 ** 31)
VMEM_LIMIT = 56 * 1024 * 1024

_C_QA, _C_KA, _C_VA, _C_QB, _C_KB, _C_VB, _C_QI, _C_KIWI = 0, 512, 1024, 1536, 2048, 2176, 2304, 2816
_IN_COLS = 2888
_IN_COLS_PAD = 2944


def _cparams(sem):
    return pltpu.CompilerParams(dimension_semantics=sem, vmem_limit_bytes=VMEM_LIMIT)


def _nt_dot(a, b):
    return lax.dot_general(a, b, (((1,), (1,)), ((), ())), preferred_element_type=F32)


def _tn_dot(a, b):
    return lax.dot_general(a, b, (((0,), (0,)), ((), ())), preferred_element_type=F32)


def _lane_iota(shape):
    return lax.broadcasted_iota(I32, shape, 1)


def _proj_kernel(x_ref, w_ref, cos_ref, sin_ref,
                 qa_ref, kaf_ref, kab_ref, vaf_ref, vab_ref, qb_ref, kbf_ref, kbb_ref, vbf_ref, vbb_ref,
                 qi_ref, kif_ref, kib_ref, wi_ref):
    xb = x_ref[...].astype(BF16)
    cos = cos_ref[...]
    sin = sin_ref[...]
    lane = _lane_iota(cos.shape)
    first = (lane % HALF) < (HALF // 2)

    def seg(c0, n):
        return jnp.dot(xb, w_ref[:, c0:c0 + n], preferred_element_type=F32)

    def rope(h):
        rot = jnp.where(first, pltpu.roll(h, LANES - HALF // 2, 1), pltpu.roll(h, HALF // 2, 1))
        return h * cos + rot * sin

    def blocks(h):
        return [h[:, k * LANES:(k + 1) * LANES] for k in range(h.shape[1] // LANES)]

    scale = A_HEAD_DIM ** -0.5 * math.log2(math.e)
    for k, h in enumerate(blocks(seg(_C_QA, 512))):
        qa_ref[:, k * LANES:(k + 1) * LANES] = (rope(h) * scale).astype(BF16)
    for k, h in enumerate(blocks(seg(_C_KA, 512))):
        r = rope(h)
        kaf_ref[:, k, :] = r
        kab_ref[:, k * LANES:(k + 1) * LANES] = r.astype(BF16)
    va = seg(_C_VA, 512)
    for k, h in enumerate(blocks(va)):
        vaf_ref[:, k, :] = h
    vab_ref[...] = va.astype(BF16)
    for k, h in enumerate(blocks(seg(_C_QB, 512))):
        qb_ref[:, k * LANES:(k + 1) * LANES] = (rope(h) * scale).astype(BF16)
    kbvb = seg(_C_KB, 256)
    kb = rope(kbvb[:, :LANES])
    kbb_ref[...] = kb.astype(BF16)
    vb = kbvb[:, LANES:]
    vbb_ref[...] = vb.astype(BF16)
    for ref, val in ((kbf_ref, kb.T), (vbf_ref, vb.T)):
        for g in range(B_KV_HEADS):
            ref[0, g] = val[g * B_HEAD_DIM:(g + 1) * B_HEAD_DIM]
    for k, h in enumerate(blocks(seg(_C_QI, 512))):
        qi_ref[:, k * LANES:(k + 1) * LANES] = rope(h).astype(BF16)
    kiwi = seg(_C_KIWI, LANES)
    ki = rope(kiwi)
    kif_ref[0] = ki.T[:IDX_DIM]
    kib_ref[...] = jnp.where(lane < HALF, ki, pltpu.roll(ki, HALF, 1)).astype(BF16)
    wi_ref[...] = pltpu.roll(kiwi, HALF, 1)[:, :IDX_HEADS]


def _project(x, w_pad, cos, sin, tm):
    n, d = x.shape
    t_tab = cos.shape[0]
    n_tab = t_tab // tm
    nseq = n // t_tab
    row = lambda i: (i, 0)
    row2 = lambda w, dt: (((n, w), dt), pl.BlockSpec((tm, w), row))
    diff_rows = (((n, A_HEADS, LANES), F32), pl.BlockSpec((tm, A_HEADS, LANES), lambda i: (i, 0, 0)))
    sparse_rows = (((nseq, B_KV_HEADS, B_HEAD_DIM, t_tab), F32),
                   pl.BlockSpec((1, B_KV_HEADS, B_HEAD_DIM, tm), lambda i: (i // n_tab, 0, 0, i % n_tab)))
    index_rows = (((nseq, IDX_DIM, t_tab), F32), pl.BlockSpec((1, IDX_DIM, tm), lambda i: (i // n_tab, 0, i % n_tab)))
    outs = [row2(512, BF16), diff_rows, row2(512, BF16), diff_rows, row2(512, BF16),
            row2(512, BF16), sparse_rows, row2(LANES, BF16), sparse_rows, row2(LANES, BF16),
            row2(512, BF16), index_rows, row2(LANES, BF16), row2(IDX_HEADS, F32)]
    return pl.pallas_call(
        _proj_kernel,
        grid=(n // tm,),
        in_specs=[pl.BlockSpec((tm, d), row),
                  pl.BlockSpec((d, _IN_COLS_PAD), lambda i: (0, 0)),
                  pl.BlockSpec((tm, LANES), lambda i: (i % n_tab, 0)),
                  pl.BlockSpec((tm, LANES), lambda i: (i % n_tab, 0))],
        out_specs=[spec for _, spec in outs],
        out_shape=[jax.ShapeDtypeStruct(s, dt) for (s, dt), _ in outs],
        compiler_params=_cparams(("parallel",)),
        name="proj",
    )(x, w_pad, cos, sin)


def _rope_tables(pos):
    inv = ROPE_THETA ** (-jnp.arange(0, HALF, 2, dtype=F32) / HALF)
    ang = pos.astype(F32)[:, None] * inv[None, :]
    cos = jnp.concatenate([jnp.cos(ang)] * 4, axis=-1)
    sin = jnp.concatenate([-jnp.sin(ang), jnp.sin(ang)] * 2, axis=-1)
    return cos, sin


def _diff_finish(acc, l, lam, g, lam_init, rows):
    o = acc[:rows] / l[:rows] - lam * (acc[rows:] / l[rows:])
    o = o * lax.rsqrt(jnp.mean(o * o, axis=-1, keepdims=True) + RMS_EPS)
    return o * g * (1.0 - lam_init)


ONES_ROWS = 16


def _with_ones_rows(vt):
    return jnp.concatenate([vt, jnp.ones(vt.shape[:-2] + (ONES_ROWS, vt.shape[-1]), vt.dtype)], axis=-2)


def _diff_kernel(lam_ref, q_ref, k_ref, vt_ref, g_ref, o_ref, m_scr, acc_scr, *, tq, lam_init):
    qi = pl.program_id(2)
    q = q_ref[0]
    lane = _lane_iota(q.shape)
    zero = jnp.zeros_like(q)
    qq = jnp.concatenate([jnp.where(lane < HALF, q, zero), jnp.where(lane >= HALF, q, zero)], axis=0)
    m_scr[...] = jnp.full(m_scr.shape, NEG_BIG, F32)
    acc_scr[...] = jnp.zeros(acc_scr.shape, F32)

    def step(j, diagonal):
        off = pl.multiple_of(j * tq, tq)
        s = _nt_dot(k_ref[0, pl.ds(off, tq), :], qq)
        if diagonal:
            kpos = lax.broadcasted_iota(I32, s.shape, 0)
            qpos = lax.broadcasted_iota(I32, s.shape, 1) % tq
            s = jnp.where(kpos <= qpos, s, NEG_BIG)
        m_old = m_scr[...]
        m_new = jnp.maximum(m_old, jnp.max(s, axis=0, keepdims=True))
        a = jnp.exp2(m_old - m_new)
        p = jnp.exp2(s - m_new)
        acc_scr[...] = a * acc_scr[...] + jnp.dot(vt_ref[0, 0, j], p.astype(BF16), preferred_element_type=F32)
        m_scr[...] = m_new

    def body(j, carry):
        step(j, False)
        return carry

    lax.fori_loop(0, qi, body, 0)
    step(qi, True)
    o = acc_scr[:LANES, :] * (1.0 / acc_scr[LANES:LANES + 1, :])
    o = o[:, :tq] - lam_ref[0, 0] * o[:, tq:]
    o = o * lax.rsqrt(jnp.mean(o * o, axis=0, keepdims=True) + RMS_EPS)
    o = o * g_ref[...] * (1.0 - lam_init)
    o_ref[0] = o.T.astype(o_ref.dtype)


def _diff_attention(lam, qa, ka, va, g, lam_init, tq):
    b, t, _ = qa.shape
    nblk = t // tq
    vt = _with_ones_rows(jnp.transpose(va.reshape(b, nblk, tq, A_HEADS, LANES), (0, 3, 1, 4, 2)))
    rows = LANES + ONES_ROWS
    return pl.pallas_call(
        functools.partial(_diff_kernel, tq=tq, lam_init=lam_init),
        grid=(b, A_HEADS, nblk),
        in_specs=[pl.BlockSpec(memory_space=pltpu.SMEM),
                  pl.BlockSpec((1, tq, LANES), lambda b_, h, i: (b_, i, h)),
                  pl.BlockSpec((1, t, LANES), lambda b_, h, i: (b_, 0, h)),
                  pl.BlockSpec((1, 1, nblk, rows, tq), lambda b_, h, i: (b_, h, 0, 0, 0)),
                  pl.BlockSpec((LANES, 1), lambda b_, h, i: (0, 0))],
        out_specs=pl.BlockSpec((1, tq, LANES), lambda b_, h, i: (b_, i, h)),
        out_shape=jax.ShapeDtypeStruct((b, t, A_HEADS * LANES), BF16),
        scratch_shapes=[pltpu.VMEM((1, 2 * tq), F32), pltpu.VMEM((rows, 2 * tq), F32)],
        compiler_params=_cparams(("parallel", "parallel", "arbitrary")),
        name="diff_attn",
    )(lam, qa, ka, vt, g.reshape(LANES, 1))


def _float_key(x):
    b = lax.bitcast_convert_type(x + 0.0, I32)
    return b ^ ((b >> 31) & 0x7FFFFFFF)


def _select_topk(key_scr, nch, ck, topk):
    nq = key_scr.shape[1]
    kf = float(topk)

    grp = 64

    def counts(tau_b, strict):
        def body(c, cnt):
            off = pl.multiple_of(c * ck, ck)
            for g in range(ck // grp):
                kk = key_scr[pl.ds(off + g * grp, grp), :]
                hit = (kk > tau_b) if strict else (kk >= tau_b)
                cnt = cnt + jnp.where(hit, 1, 0)
            return cnt
        cnt = lax.fori_loop(0, nch, body, jnp.zeros((grp, nq), I32))
        return jnp.sum(cnt.astype(F32), axis=0, keepdims=True)

    def bcast(v, rows=grp):
        return jnp.broadcast_to(v, (rows, nq))

    cnt0 = counts(jnp.zeros((grp, nq), I32), False)
    tau = jnp.where(cnt0 >= kf, 0, INT_MIN).astype(I32)

    def bit_body(i, tau):
        cand = tau | jnp.left_shift(jnp.int32(1), 30 - i)
        cnt = counts(bcast(cand), False)
        return jnp.where(cnt >= kf, cand, tau)

    tau = lax.fori_loop(0, 31, bit_body, tau)
    cnt_ge = counts(bcast(tau), False)
    cnt_gt = counts(bcast(tau), True)
    overfull = jnp.where((cnt_ge > kf) & (tau > INT_MIN), 1.0, 0.0)

    @pl.when(jnp.max(overfull) > 0.0)
    def _():
        tau_b = bcast(tau, LANES)
        need_b = bcast(kf - cnt_gt, LANES)
        lower = jnp.where(lax.broadcasted_iota(I32, (LANES, LANES), 0)
                          >= lax.broadcasted_iota(I32, (LANES, LANES), 1), 1.0, 0.0).astype(BF16)

        def body(c, run):
            off = pl.multiple_of(c * ck, ck)
            for g in range(ck // LANES):
                sl = pl.ds(off + g * LANES, LANES)
                kk = key_scr[sl, :]
                eq = kk == tau_b
                eqf = jnp.where(eq, 1.0, 0.0)
                pre = jnp.dot(lower, eqf.astype(BF16), preferred_element_type=F32)
                drop = eq & ((pre + run) > need_b)
                key_scr[sl, :] = jnp.where(drop, INT_MIN, kk)
                run = run + jnp.sum(eqf, axis=0, keepdims=True)
            return run

        lax.fori_loop(0, nch, body, jnp.zeros((1, nq), F32))

    return tau


def _stack_heads(q, n_heads):
    lane = _lane_iota((q.shape[0], LANES))
    zero = jnp.zeros((q.shape[0], LANES), q.dtype)
    parts = []
    for h in range(n_heads):
        blk = q[:, (h // 2) * LANES:(h // 2 + 1) * LANES]
        parts.append(jnp.where((lane < HALF) == (h % 2 == 0), blk, zero))
    return jnp.concatenate(parts, axis=0)


M_INIT = -1e29


def _sparse_kernel(qi_ref, wt_ref, ki_ref, qb_ref, kb_ref, vt_ref, o_ref, key_scr, m_scr, acc_scr,
                   *, tq, ck, topk):
    i = pl.program_id(1)
    nch = ((i + 1) * tq + ck - 1) // ck
    kpos0 = lax.broadcasted_iota(I32, (ck, tq), 0)
    qpos = i * tq + lax.broadcasted_iota(I32, (ck, tq), 1)

    qs = _stack_heads(qi_ref[0], IDX_HEADS)
    wt = wt_ref[0]

    def score_body(c, carry):
        off = pl.multiple_of(c * ck, ck)
        d = _nt_dot(ki_ref[0, pl.ds(off, ck), :], qs)
        sc = jnp.zeros((ck, tq), F32)
        for h in range(IDX_HEADS):
            sc = sc + jnp.maximum(d[:, h * tq:(h + 1) * tq], 0.0) * wt[h:h + 1, :]
        sc = sc * IDX_SCALE
        key_scr[pl.ds(off, ck), :] = jnp.where(kpos0 + off <= qpos, _float_key(sc), INT_MIN)
        return carry

    lax.fori_loop(0, nch, score_body, 0)
    tau = _select_topk(key_scr, nch, ck, topk)
    tau_b = jnp.broadcast_to(tau, (ck, tq))

    qh = []
    qb = qb_ref[0]
    lane = _lane_iota((tq, LANES))
    zero = jnp.zeros((tq, LANES), qb.dtype)
    grp = B_HEADS // B_KV_HEADS
    for h in range(B_HEADS):
        blk = qb[:, (h // 2) * LANES:(h // 2 + 1) * LANES]
        if h % 2 != h // grp:
            blk = pltpu.roll(blk, HALF, 1)
        qh.append(jnp.where((lane >= HALF) == (h // grp == 1), blk, zero))
    qs2 = jnp.concatenate(qh, axis=0)
    m_scr[...] = jnp.full(m_scr.shape, M_INIT, F32)
    acc_scr[...] = jnp.zeros(acc_scr.shape, F32)

    def attn_body(c, carry):
        off = pl.multiple_of(c * ck, ck)
        sel = (key_scr[pl.ds(off, ck), :] >= tau_b) & (kpos0 + off <= qpos)
        bias = jnp.where(sel, 0.0, NEG_BIG)
        s = _nt_dot(kb_ref[0, pl.ds(off, ck), :], qs2)
        ps, scales = [], []
        for h in range(B_HEADS):
            hs = slice(h * tq, (h + 1) * tq)
            sh = s[:, hs] + bias
            m_old = m_scr[:, hs]
            m_new = jnp.maximum(m_old, jnp.max(sh, axis=0, keepdims=True))
            a = jnp.exp2(m_old - m_new)
            p = jnp.exp2(sh - m_new)
            m_scr[:, hs] = m_new
            ps.append(p.astype(BF16))
            scales.append(a)
        pv = jnp.dot(vt_ref[0, c], jnp.concatenate(ps, axis=1), preferred_element_type=F32)
        acc_scr[...] = jnp.concatenate(scales, axis=1) * acc_scr[...] + pv
        return carry

    lax.fori_loop(0, nch, attn_body, 0)
    o = acc_scr[:LANES, :] * (1.0 / acc_scr[LANES:LANES + 1, :])
    parts = [o[(h // grp) * HALF:(h // grp + 1) * HALF, h * tq:(h + 1) * tq] for h in range(B_HEADS)]
    o_ref[0] = jnp.concatenate(parts, axis=0).T.astype(o_ref.dtype)


def _sparse_attention(qi, wi, ki2, qb, kb, vb, tq, ck):
    b, t, _ = qi.shape
    topk = min(TOPK_MAX, t // 4)
    wt = jnp.transpose(wi, (0, 2, 1))
    vt = _with_ones_rows(jnp.transpose(vb.reshape(b, t // ck, ck, LANES), (0, 1, 3, 2)))
    rows = LANES + ONES_ROWS
    blk_q = lambda w: pl.BlockSpec((1, tq, w), lambda b_, i: (b_, i, 0))
    blk_k = pl.BlockSpec((1, t, LANES), lambda b_, i: (b_, 0, 0))
    return pl.pallas_call(
        functools.partial(_sparse_kernel, tq=tq, ck=ck, topk=topk),
        grid=(b, t // tq),
        in_specs=[blk_q(512), pl.BlockSpec((1, IDX_HEADS, tq), lambda b_, i: (b_, 0, i)), blk_k, blk_q(512), blk_k,
                  pl.BlockSpec((1, t // ck, rows, ck), lambda b_, i: (b_, 0, 0, 0))],
        out_specs=blk_q(512),
        out_shape=jax.ShapeDtypeStruct((b, t, 512), BF16),
        scratch_shapes=[pltpu.VMEM((t, tq), I32), pltpu.VMEM((1, B_HEADS * tq), F32),
                        pltpu.VMEM((rows, B_HEADS * tq), F32)],
        compiler_params=_cparams(("parallel", "arbitrary")),
        name="sparse_attn",
    )(qi, wt, ki2, qb, kb, vt)


def _page_specs(n, block, pg):
    zeros = (0,) * (len(block) - 1)
    return [pl.BlockSpec(block, functools.partial(lambda i, b, j, pt: (pt[b, j * pg + i],) + zeros, i))
            for i in range(n)]


def _dec_diff_kernel(pt_ref, lam_ref, q_ref, kn_ref, vn_ref, g_ref, qi_ref, w_ref, kin_ref, *rest, pg, lam_init):
    k_refs, v_refs, ki_refs = rest[:pg], rest[pg:2 * pg], rest[2 * pg:3 * pg]
    o_ref, sc_ref, new_ref, m_scr, l_scr, acc_scr, kbuf, vbuf, kibuf = rest[3 * pg:]
    j = pl.program_id(1)

    @pl.when(j == 0)
    def _():
        m_scr[...] = jnp.full(m_scr.shape, NEG_BIG, F32)
        l_scr[...] = jnp.zeros(l_scr.shape, F32)
        acc_scr[...] = jnp.zeros(acc_scr.shape, F32)

    rows_pg = k_refs[0].shape[1]
    page = ki_refs[0].shape[2]
    for i in range(pg):
        kbuf[i * rows_pg:(i + 1) * rows_pg, :] = k_refs[i][0].astype(BF16)
        vbuf[i * rows_pg:(i + 1) * rows_pg, :] = v_refs[i][0].astype(BF16)
        kibuf[:, i * page:(i + 1) * page] = ki_refs[i][0].astype(BF16)

    q = q_ref[0]
    s = _nt_dot(q, kbuf[...])
    same_head = (_lane_iota(s.shape) % A_HEADS) == (lax.broadcasted_iota(I32, s.shape, 0) % A_HEADS)
    s = jnp.where(same_head, s, NEG_BIG)
    m_old = m_scr[...]
    m_new = jnp.maximum(m_old, jnp.max(s, axis=1, keepdims=True))
    a = jnp.exp2(m_old - m_new)
    p = jnp.exp2(s - m_new).astype(BF16)
    l_scr[...] = a * l_scr[...] + jnp.sum(p.astype(F32), axis=1, keepdims=True)
    acc_scr[...] = a * acc_scr[...] + jnp.dot(p, vbuf[...], preferred_element_type=F32)
    m_scr[...] = m_new

    qi = qi_ref[0]
    w = w_ref[0]
    d = jnp.dot(qi, kibuf[...], preferred_element_type=F32)
    sc_ref[0] = jnp.sum(jnp.maximum(d, 0.0) * w, axis=0, keepdims=True) * IDX_SCALE

    @pl.when(j == pl.num_programs(1) - 1)
    def _():
        s_new = jnp.sum(q.astype(F32) * kn_ref[0], axis=1, keepdims=True)
        m_old = m_scr[...]
        m_new = jnp.maximum(m_old, s_new)
        a = jnp.exp2(m_old - m_new)
        pn = jnp.exp2(s_new - m_new)
        l = a * l_scr[...] + pn
        acc = a * acc_scr[...] + pn * vn_ref[0]
        o = _diff_finish(acc, l, lam_ref[0, 0], g_ref[...], lam_init, A_HEADS)
        o_ref[0] = jnp.concatenate([o, jnp.zeros_like(o)], axis=0)
        dn = jnp.sum(qi.astype(F32) * kin_ref[0], axis=1, keepdims=True)
        scn = jnp.sum(jnp.maximum(dn, 0.0) * w, axis=0, keepdims=True) * IDX_SCALE
        new_ref[0] = jnp.broadcast_to(scn, new_ref.shape[1:])


def _dec_diff_attention(pt, lam, qm, knew, vnew, g, qi, wi, kinew, kpages, vpages, kipages, lam_init, pg):
    db, npg = pt.shape
    rows_pg = kpages.shape[1]
    page = kipages.shape[2]
    per_b = lambda shape: pl.BlockSpec(shape, lambda b, j, pt_: (b,) + (0,) * (len(shape) - 1))
    return pl.pallas_call(
        functools.partial(_dec_diff_kernel, pg=pg, lam_init=lam_init),
        grid_spec=pltpu.PrefetchScalarGridSpec(
            num_scalar_prefetch=1,
            grid=(db, npg // pg),
            in_specs=[pl.BlockSpec(memory_space=pltpu.SMEM), per_b((1, 8, LANES)), per_b((1, 8, LANES)),
                      per_b((1, 8, LANES)), pl.BlockSpec((1, LANES), lambda b, j, pt_: (0, 0)),
                      per_b((1, IDX_HEADS, IDX_DIM)), per_b((1, IDX_HEADS, 1)), per_b((1, 1, IDX_DIM))]
                     + _page_specs(pg, (1,) + kpages.shape[1:], pg) + _page_specs(pg, (1,) + vpages.shape[1:], pg)
                     + _page_specs(pg, (1,) + kipages.shape[1:], pg),
            out_specs=[per_b((1, 8, LANES)), pl.BlockSpec((1, 1, pg * page), lambda b, j, pt_: (b, 0, j)),
                       per_b((1, 1, LANES))],
            scratch_shapes=[pltpu.VMEM((8, 1), F32), pltpu.VMEM((8, 1), F32), pltpu.VMEM((8, LANES), F32),
                            pltpu.VMEM((pg * rows_pg, LANES), BF16), pltpu.VMEM((pg * rows_pg, LANES), BF16),
                            pltpu.VMEM((IDX_DIM, pg * page), BF16)],
        ),
        out_shape=[jax.ShapeDtypeStruct((db, 8, LANES), F32), jax.ShapeDtypeStruct((db, 1, npg * page), F32),
                   jax.ShapeDtypeStruct((db, 1, LANES), F32)],
        compiler_params=_cparams(("parallel", "arbitrary")),
        name="dec_diff_attn",
    )(pt, lam, qm, knew, vnew, g, qi, wi, kinew, *([kpages] * pg), *([vpages] * pg), *([kipages] * pg))


def _dec_select_kernel(sc_ref, key_ref, tau_ref, *, n_keys, topk):
    width, nq = sc_ref.shape
    kpos = lax.broadcasted_iota(I32, (width, nq), 0)
    key_ref[...] = jnp.where(kpos < n_keys, _float_key(sc_ref[...]), INT_MIN)
    tau = _select_topk(key_ref, width // LANES, LANES, topk)
    tau_ref[...] = jnp.broadcast_to(tau, tau_ref.shape)


def _dec_select(scores_t, n_keys, topk):
    width, nq = scores_t.shape
    return pl.pallas_call(
        functools.partial(_dec_select_kernel, n_keys=n_keys, topk=topk),
        out_shape=[jax.ShapeDtypeStruct((width, nq), I32), jax.ShapeDtypeStruct((8, nq), I32)],
        compiler_params=pltpu.CompilerParams(vmem_limit_bytes=VMEM_LIMIT),
        name="dec_select",
    )(scores_t)


def _dec_sparse_kernel(pt_ref, q_ref, key_ref, keyn_ref, tau_ref, kn_ref, vn_ref, *rest, pg):
    k_refs, v_refs = rest[:pg], rest[pg:2 * pg]
    o_ref, m_scr, l_scr, acc_scr, kbuf, vbuf = rest[2 * pg:]
    j = pl.program_id(1)

    @pl.when(j == 0)
    def _():
        m_scr[...] = jnp.full(m_scr.shape, NEG_BIG, F32)
        l_scr[...] = jnp.zeros(l_scr.shape, F32)
        acc_scr[...] = jnp.zeros(acc_scr.shape, F32)

    page = k_refs[0].shape[3]
    for i in range(pg):
        kbuf[:, :, i * page:(i + 1) * page] = k_refs[i][0].astype(BF16)
        vbuf[:, :, i * page:(i + 1) * page] = v_refs[i][0].astype(BF16)
    tau = tau_ref[0][:, 0:1]
    sel = jnp.broadcast_to(key_ref[0] >= tau, (8, pg * page))
    for g in range(B_KV_HEADS):
        q = q_ref[0, g]
        s = jnp.dot(q, kbuf[g], preferred_element_type=F32)
        s = jnp.where(sel, s, NEG_BIG)
        m_old = m_scr[g]
        m_new = jnp.maximum(m_old, jnp.max(s, axis=1, keepdims=True))
        a = jnp.exp2(m_old - m_new)
        p = jnp.where(sel, jnp.exp2(s - m_new), 0.0).astype(BF16)
        l_scr[g] = a * l_scr[g] + jnp.sum(p.astype(F32), axis=1, keepdims=True)
        acc_scr[g] = a * acc_scr[g] + _nt_dot(p, vbuf[g])
        m_scr[g] = m_new

    @pl.when(j == pl.num_programs(1) - 1)
    def _():
        sel_new = keyn_ref[0][:, 0:1] >= tau
        for g in range(B_KV_HEADS):
            s_new = jnp.sum(q_ref[0, g].astype(F32) * kn_ref[0, g], axis=1, keepdims=True)
            s_new = jnp.where(sel_new, s_new, NEG_BIG)
            m_old = m_scr[g]
            m_new = jnp.maximum(m_old, s_new)
            a = jnp.exp2(m_old - m_new)
            pn = jnp.where(sel_new, jnp.exp2(s_new - m_new), 0.0)
            l = a * l_scr[g] + pn
            o_ref[0, g] = (a * acc_scr[g] + pn * vn_ref[0, g]) / l


def _dec_sparse_attention(pt, q, keys, tau, kbnew, vbnew, kpages, vpages, pg):
    db, npg = pt.shape
    page = kpages.shape[3]
    per_b = lambda shape: pl.BlockSpec(shape, lambda b, j, pt_: (b,) + (0,) * (len(shape) - 1))
    blk = (1,) + kpages.shape[1:]
    return pl.pallas_call(
        functools.partial(_dec_sparse_kernel, pg=pg),
        grid_spec=pltpu.PrefetchScalarGridSpec(
            num_scalar_prefetch=1,
            grid=(db, npg // pg),
            in_specs=[per_b((1, B_KV_HEADS, 8, B_HEAD_DIM)),
                      pl.BlockSpec((1, 1, pg * page), lambda b, j, pt_: (b, 0, j)),
                      pl.BlockSpec((1, 1, LANES), lambda b, j, pt_: (b, 0, npg * page // LANES)),
                      per_b((1, 1, LANES)), per_b((1, B_KV_HEADS, 1, B_HEAD_DIM)),
                      per_b((1, B_KV_HEADS, 1, B_HEAD_DIM))]
                     + _page_specs(pg, blk, pg) + _page_specs(pg, blk, pg),
            out_specs=per_b((1, B_KV_HEADS, 8, B_HEAD_DIM)),
            scratch_shapes=[pltpu.VMEM((B_KV_HEADS, 8, 1), F32), pltpu.VMEM((B_KV_HEADS, 8, 1), F32),
                            pltpu.VMEM((B_KV_HEADS, 8, B_HEAD_DIM), F32),
                            pltpu.VMEM((B_KV_HEADS, B_HEAD_DIM, pg * page), BF16),
                            pltpu.VMEM((B_KV_HEADS, B_HEAD_DIM, pg * page), BF16)],
        ),
        out_shape=jax.ShapeDtypeStruct((db, B_KV_HEADS, 8, B_HEAD_DIM), F32),
        compiler_params=_cparams(("parallel", "arbitrary")),
        name="dec_sparse_attn",
    )(pt, q, keys, keys, tau, kbnew, vbnew, *([kpages] * pg), *([vpages] * pg))


def _layer_norm(x, g, b):
    mu = jnp.mean(x, axis=-1, keepdims=True)
    xc = x - mu
    var = jnp.mean(xc * xc, axis=-1, keepdims=True)
    return xc * lax.rsqrt(var + LN_EPS) * g + b


def _post_kernel(x_ref, ma_ref, mb_ref, woa_ref, wob_ref, g_ref, b_ref, rw_ref, rb_ref, h_ref, c_ref):
    y = (DEEPNORM_ALPHA * x_ref[...]
         + jnp.dot(ma_ref[...], woa_ref[...], preferred_element_type=F32)
         + jnp.dot(mb_ref[...], wob_ref[...], preferred_element_type=F32))
    h = _layer_norm(y, g_ref[...], b_ref[...])
    h_ref[...] = h
    logits = jnp.dot(h, rw_ref[...], preferred_element_type=F32, precision=lax.Precision.HIGHEST) + rb_ref[...]
    lane = _lane_iota(logits.shape)
    work = logits
    sel = jnp.zeros(logits.shape, jnp.bool_)
    for _ in range(TOP_K):
        mx = jnp.max(work, axis=-1, keepdims=True)
        idx = jnp.min(jnp.where(work == mx, lane, N_EXPERTS), axis=-1, keepdims=True)
        pick = lane == idx
        sel = sel | pick
        work = jnp.where(pick, -jnp.inf, work)
    e = jnp.where(sel, jnp.exp(logits - jnp.max(logits, axis=-1, keepdims=True)), 0.0)
    c_ref[...] = e / jnp.sum(e, axis=-1, keepdims=True)


def _post_mixer(x, mix_a, mix_b, wo_a, wo_b, g, b, rw, rb, tm):
    n, d = x.shape
    row = lambda w: pl.BlockSpec((tm, w), lambda i: (i, 0))
    full = lambda a: pl.BlockSpec(a.shape, lambda i: (0, 0))
    return pl.pallas_call(
        _post_kernel,
        grid=(n // tm,),
        in_specs=[row(d), row(mix_a.shape[1]), row(mix_b.shape[1]), full(wo_a), full(wo_b), full(g), full(b),
                  full(rw), full(rb)],
        out_specs=[row(d), row(N_EXPERTS)],
        out_shape=[jax.ShapeDtypeStruct((n, d), F32), jax.ShapeDtypeStruct((n, N_EXPERTS), F32)],
        compiler_params=_cparams(("parallel",)),
        name="post_mixer",
    )(x, mix_a, mix_b, wo_a, wo_b, g, b, rw, rb)


def _moe_kernel(cnt_ref, h_ref, comb_ref, upper_ref, wg_ref, bg_ref, wu_ref, bu_ref, wd_ref, bd_ref,
                g_ref, b_ref, o_ref, xs_scr, pm_scr, gs_scr, ys_scr, *, tt, cs, nsub):
    i = pl.program_id(0)
    e = pl.program_id(1)
    n_exp = pl.num_programs(1)

    @pl.when(e == 0)
    def _():
        o_ref[...] = jnp.zeros(o_ref.shape, F32)

    slot0 = lax.broadcasted_iota(I32, (cs, tt), 0).astype(F32) + 1.0

    def onehot_of(j, first):
        gate = comb_ref[0, 0, pl.ds(j, 1), :]
        sel = gate > 0.0
        self_ = jnp.broadcast_to(jnp.where(sel, 1.0, 0.0), (8, tt)).astype(BF16)
        rank = jnp.dot(self_, upper_ref[...], preferred_element_type=F32)[0:1]
        onehot = jnp.broadcast_to(sel, (cs, tt)) & (jnp.broadcast_to(rank, (cs, tt)) == slot0 + first)
        return onehot, jnp.broadcast_to(gate, (cs, tt))

    def expert(xs):
        gg = jnp.dot(xs, wg_ref[0], preferred_element_type=F32) + bg_ref[0]
        uu = jnp.dot(xs, wu_ref[0], preferred_element_type=F32) + bu_ref[0]
        gg = jnp.minimum(gg, SWIGLU_LIMIT)
        uu = jnp.clip(uu, -SWIGLU_LIMIT, SWIGLU_LIMIT)
        hh = gg * jax.nn.sigmoid(SWIGLU_ALPHA * gg) * (uu + 1.0)
        return jnp.dot(hh.astype(BF16), wd_ref[0], preferred_element_type=F32) + bd_ref[0]

    for j in range(nsub):
        onehot, gate_b = onehot_of(j, 0.0)
        pm = jnp.where(onehot, 1.0, 0.0).astype(BF16)
        pm_scr[j] = pm
        xs_scr[j * cs:(j + 1) * cs, :] = jnp.dot(pm, h_ref[j * tt:(j + 1) * tt, :].astype(BF16),
                                                 preferred_element_type=F32).astype(BF16)
        gs_scr[j * cs:(j + 1) * cs, :] = jnp.sum(jnp.where(onehot, gate_b, 0.0), axis=1, keepdims=True)
    ys_scr[...] = (expert(xs_scr[...]) * gs_scr[...]).astype(BF16)
    for j in range(nsub):
        o_ref[j * tt:(j + 1) * tt, :] += _tn_dot(pm_scr[j], ys_scr[j * cs:(j + 1) * cs, :])

    def sub_body(j, carry):
        n = cnt_ref[(e * pl.num_programs(0) + i) * nsub + j]

        @pl.when(n > cs)
        def _():
            off = pl.multiple_of(j * tt, tt)
            xj = h_ref[pl.ds(off, tt), :].astype(BF16)

            def chunk_body(c, carry2):
                onehot, gate_b = onehot_of(j, (c * cs).astype(F32))
                pm = jnp.where(onehot, 1.0, 0.0).astype(BF16)
                y = expert(jnp.dot(pm, xj, preferred_element_type=F32).astype(BF16))
                gs = jnp.sum(jnp.where(onehot, gate_b, 0.0), axis=1, keepdims=True)
                o_ref[pl.ds(off, tt), :] += _tn_dot(pm, (y * gs).astype(BF16))
                return carry2

            lax.fori_loop(1, (n + cs - 1) // cs, chunk_body, 0)

        return carry

    lax.fori_loop(0, nsub, sub_body, 0)

    @pl.when(e == n_exp - 1)
    def _():
        o_ref[...] = _layer_norm(DEEPNORM_ALPHA * h_ref[...] + o_ref[...], g_ref[...], b_ref[...])


def _moe(h, comb, wg, bg, wu, bu, wd, bd, g, b, tb, tt, cs):
    n, d = h.shape
    n_exp, _, f = wg.shape
    nsub = tb // tt
    nblk = n // tb
    comb_t = comb.T
    counts = jnp.sum((comb_t > 0.0).reshape(n_exp * nblk * nsub, tt), axis=-1).astype(I32)
    comb_t = comb_t.reshape(n_exp, nblk, nsub, tt)
    upper = (jnp.arange(tt)[:, None] <= jnp.arange(tt)[None, :]).astype(BF16)
    wspec = lambda a, b_: pl.BlockSpec((1, a, b_), lambda i, e, c: (e, 0, 0))
    vec = pl.BlockSpec((1, d), lambda i, e, c: (0, 0))
    return pl.pallas_call(
        functools.partial(_moe_kernel, tt=tt, cs=cs, nsub=nsub),
        grid_spec=pltpu.PrefetchScalarGridSpec(
            num_scalar_prefetch=1,
            grid=(nblk, n_exp),
            in_specs=[pl.BlockSpec((tb, d), lambda i, e, c: (i, 0), pipeline_mode=pl.Buffered(1)),
                      pl.BlockSpec((1, 1, nsub, tt), lambda i, e, c: (e, i, 0, 0)),
                      pl.BlockSpec((tt, tt), lambda i, e, c: (0, 0)),
                      wspec(d, f), wspec(1, f), wspec(d, f), wspec(1, f), wspec(f, d), wspec(1, d), vec, vec],
            out_specs=pl.BlockSpec((tb, d), lambda i, e, c: (i, 0)),
            scratch_shapes=[pltpu.VMEM((nsub * cs, d), BF16), pltpu.VMEM((nsub, cs, tt), BF16),
                            pltpu.VMEM((nsub * cs, 1), F32), pltpu.VMEM((nsub * cs, d), BF16)],
        ),
        out_shape=jax.ShapeDtypeStruct((n, d), F32),
        compiler_params=_cparams(("parallel", "arbitrary")),
        name="moe",
    )(counts, h, comb_t, upper, wg, bg.reshape(n_exp, 1, f), wu, bu.reshape(n_exp, 1, f),
      wd, bd.reshape(n_exp, 1, d), g, b)


def _pad_w_in(w_in):
    d = w_in.shape[0]
    return jnp.concatenate([w_in, jnp.zeros((d, _IN_COLS_PAD - _IN_COLS), w_in.dtype)], axis=1).astype(BF16)


def _prep_params(p, l=0):
    half = p["w_out"].shape[1] // 2
    vec = lambda a: a[l].reshape(1, -1)
    return {
        "wo_a": p["w_out"][l, :half].astype(BF16), "wo_b": p["w_out"][l, half:].astype(BF16),
        "ln1_g": vec(p["ln1_g"]), "ln1_b": vec(p["ln1_b"]), "ln2_g": vec(p["ln2_g"]), "ln2_b": vec(p["ln2_b"]),
        "router_w": p["router_w"][l], "router_b": vec(p["router_b"]),
        "w_gate": p["w_gate"][l].astype(BF16), "b_gate": p["b_gate"][l],
        "w_up": p["w_up"][l].astype(BF16), "b_up": p["b_up"][l],
        "w_down": p["w_down"][l].astype(BF16), "b_down": p["b_down"][l],
    }


def _pick_tile(n, pref):
    t = min(n, pref)
    while n % t:
        t //= 2
    return t


def _token_major(a):
    a = jnp.moveaxis(a, -1, 1)
    return a.reshape((a.shape[0] * a.shape[1],) + a.shape[2:])


def _cache_rows(kaf, vaf, kbf, vbf, kif, b, t):
    return (kaf.reshape(1, b, t, A_HEADS, 2 * A_HEAD_DIM), vaf.reshape(1, b, t, A_HEADS, 2 * A_HEAD_DIM),
            _token_major(kbf).reshape(1, b, t, B_KV_HEADS, B_HEAD_DIM),
            _token_major(vbf).reshape(1, b, t, B_KV_HEADS, B_HEAD_DIM),
            _token_major(kif).reshape(1, b, t, IDX_DIM))


def _prompt_mixers(x_prompt, w_pad, lam, subln_g, lam_init):
    b, t, d = x_prompt.shape
    n = b * t
    cos, sin = _rope_tables(jnp.arange(t))
    (qa, kaf, kab, vaf, vab, qb, kbf, kbb, vbf, vbb, qi, kif, kib, wi) = _project(
        x_prompt.reshape(n, d), w_pad, cos, sin, _pick_tile(t, 512))
    r3 = lambda a: a.reshape(b, t, a.shape[-1])
    mix_a = _diff_attention(lam, r3(qa), r3(kab), r3(vab), subln_g, lam_init, _pick_tile(t, 1024))
    mix_b = _sparse_attention(r3(qi), r3(wi), r3(kib), r3(qb), r3(kbb), r3(vbb), _pick_tile(t, 512),
                              _pick_tile(t, 512))
    return mix_a.reshape(n, -1), mix_b.reshape(n, -1), _cache_rows(kaf, vaf, kbf, vbf, kif, b, t)


def _decode_mixers(x_sample, w_pad, lam, subln_g, lam_init, caches, page_table, pg):
    ck_diff, cv_diff, ck_sparse, cv_sparse, ck_index = caches
    db, _, d = x_sample.shape
    n_pool, page = ck_diff.shape[:2]
    npg = page_table.shape[1]
    past = npg * page
    cos, sin = _rope_tables(jnp.full((db,), past))
    (qa, kaf, _, vaf, _, qb, kbf, _, vbf, _, qi, kif, _, wi) = _project(x_sample.reshape(db, d), w_pad, cos, sin, db)

    qa3 = qa.reshape(db, A_HEADS, LANES)
    lane = _lane_iota(qa3.shape[1:])[None]
    zero = jnp.zeros_like(qa3)
    qm = jnp.concatenate([jnp.where(lane < HALF, qa3, zero), jnp.where(lane >= HALF, qa3, zero)], axis=1)
    twice = lambda a: jnp.concatenate([a.reshape(db, A_HEADS, LANES)] * 2, axis=1)
    out_a, scores, sc_new = _dec_diff_attention(
        page_table, lam, qm, twice(kaf), twice(vaf), subln_g,
        qi.reshape(db, IDX_HEADS, IDX_DIM), wi.reshape(db, IDX_HEADS, 1), _token_major(kif).reshape(db, 1, IDX_DIM),
        ck_diff.reshape(n_pool, page * A_HEADS, LANES), cv_diff.reshape(n_pool, page * A_HEADS, LANES),
        jnp.transpose(ck_index, (0, 2, 1)), lam_init, pg)
    mix_a = out_a[:, :A_HEADS].reshape(db, A_HEADS * LANES).astype(BF16)
    sc_all = jnp.concatenate([scores[:, 0], sc_new[:, 0, :1], jnp.zeros((db, LANES - 1), F32)], axis=1)
    keys_t, tau = _dec_select(sc_all.T, past + 1, min(TOPK_MAX, (past + 1) // 4))
    keys = keys_t.T
    tau = jnp.broadcast_to(tau[0].reshape(db, 1, 1), (db, 1, LANES))

    grp = B_HEADS // B_KV_HEADS
    qb4 = qb.reshape(db, B_KV_HEADS, grp, B_HEAD_DIM)
    qs = jnp.concatenate([qb4, jnp.zeros((db, B_KV_HEADS, 8 - grp, B_HEAD_DIM), qb.dtype)], axis=2)
    out_b = _dec_sparse_attention(page_table, qs, keys.reshape(db, 1, -1), tau,
                                  _token_major(kbf).reshape(db, B_KV_HEADS, 1, B_HEAD_DIM),
                                  _token_major(vbf).reshape(db, B_KV_HEADS, 1, B_HEAD_DIM),
                                  jnp.transpose(ck_sparse, (0, 2, 3, 1)), jnp.transpose(cv_sparse, (0, 2, 3, 1)),
                                  _pick_tile(npg, 2 * pg))
    mix_b = out_b[:, :, :grp].reshape(db, B_HEADS * B_HEAD_DIM).astype(BF16)
    return mix_a, mix_b, _cache_rows(kaf, vaf, kbf, vbf, kif, db, 1)


def _slot_rows(tt):
    pack = 16
    want = tt * TOP_K // N_EXPERTS
    want += want // 2
    return max(pack, -(-want // pack) * pack)


def _post_layer(x, mix_a, mix_b, p, tb, tt):
    n, d = x.shape
    h, comb = _post_mixer(x, mix_a, mix_b, p["wo_a"], p["wo_b"], p["ln1_g"], p["ln1_b"], p["router_w"],
                          p["router_b"], _pick_tile(n, 512))
    return _moe(h, comb, p["w_gate"], p["b_gate"], p["w_up"], p["b_up"], p["w_down"], p["b_down"],
                p["ln2_g"], p["ln2_b"], tb, tt, _slot_rows(tt))


def kernel(x_prompt, x_sample, cache_k_diff, cache_v_diff, cache_k_sparse, cache_v_sparse, cache_k_index, page_table,
           w_in, lambda_q1, lambda_k1, lambda_q2, lambda_k2, subln_g, w_out, ln1_g, ln1_b, router_w, router_b,
           w_gate, b_gate, w_up, b_up, w_down, b_down, ln2_g, ln2_b):
    assert w_in.shape[0] == DEPTH
    l = 0
    b, t, d = x_prompt.shape
    db = x_sample.shape[0]
    lam_init = 0.8 - 0.6 * math.exp(-0.3 * l)
    lam = (jnp.exp(jnp.sum(lambda_q1[l] * lambda_k1[l])) - jnp.exp(jnp.sum(lambda_q2[l] * lambda_k2[l]))
           + lam_init).reshape(1, 1)
    g = subln_g[l].reshape(1, -1)
    w_pad = _pad_w_in(w_in[l])
    params = _prep_params(dict(w_out=w_out, ln1_g=ln1_g, ln1_b=ln1_b, router_w=router_w, router_b=router_b,
                               w_gate=w_gate, b_gate=b_gate, w_up=w_up, b_up=b_up, w_down=w_down, b_down=b_down,
                               ln2_g=ln2_g, ln2_b=ln2_b), l)

    mix_a, mix_b, rows_p = _prompt_mixers(x_prompt, w_pad, lam, g, lam_init)
    n = b * t
    y_p = _post_layer(x_prompt.reshape(n, d), mix_a, mix_b, params, _pick_tile(n, 2048), _pick_tile(n, 512))

    caches = (cache_k_diff[l], cache_v_diff[l], cache_k_sparse[l], cache_v_sparse[l], cache_k_index[l])
    mix_a, mix_b, rows_s = _decode_mixers(x_sample, w_pad, lam, g, lam_init, caches, page_table,
                                          _pick_tile(page_table.shape[1], 16))
    y_s = _post_layer(x_sample.reshape(db, d), mix_a, mix_b, params, db, db)
    return (y_p.reshape(b, t, d), y_s.reshape(db, 1, d)) + rows_p + rows_s
```

```python
import functools
import math

import jax
import jax.numpy as jnp
from jax import lax
from jax.experimental import pallas as pl
from jax.experimental.pallas import tpu as pltpu

F32 = jnp.float32
BF16 = jnp.bfloat16
I32 = jnp.int32

A_HEADS = 4
A_HEAD_DIM = 64
B_HEADS = 8
B_KV_HEADS = 2
B_HEAD_DIM = 64
IDX_HEADS = 8
IDX_DIM = 64
TOPK_MAX = 256
N_EXPERTS = 32
TOP_K = 4
SWIGLU_LIMIT = 7.0
SWIGLU_ALPHA = 1.702
ROPE_THETA = 10000.0
LN_EPS = 1e-5
RMS_EPS = 1e-5
DEPTH = 1
DEEPNORM_ALPHA = (2.0 * DEPTH) ** 0.25
IDX_SCALE = (IDX_HEADS * IDX_DIM) ** -0.5

LANES = 128
HALF = 64
NEG_BIG = -1e30
INT_MIN = -(2 ** 31)
VMEM_LIMIT = 56 * 1024 * 1024

_C_QA, _C_KA, _C_VA, _C_QB, _C_KB, _C_VB, _C_QI, _C_KIWI = 0, 512, 1024, 1536, 2048, 2176, 2304, 2816
_IN_COLS = 2888
_IN_COLS_PAD = 2944


def _cparams(sem):
    return pltpu.CompilerParams(dimension_semantics=sem, vmem_limit_bytes=VMEM_LIMIT)


def _nt_dot(a, b):
    return lax.dot_general(a, b, (((1,), (1,)), ((), ())), preferred_element_type=F32)


def _tn_dot(a, b):
    return lax.dot_general(a, b, (((0,), (0,)), ((), ())), preferred_element_type=F32)


def _lane_iota(shape):
    return lax.broadcasted_iota(I32, shape, 1)


def _proj_kernel(x_ref, w_ref, cos_ref, sin_ref,
                 qa_ref, kaf_ref, kab_ref, vaf_ref, vab_ref, qb_ref, kbf_ref, kbb_ref, vbf_ref, vbb_ref,
                 qi_ref, kif_ref, kib_ref, wi_ref):
    xb = x_ref[...].astype(BF16)
    cos = cos_ref[...]
    sin = sin_ref[...]
    lane = _lane_iota(cos.shape)
    first = (lane % HALF) < (HALF // 2)

    def seg(c0, n):
        return jnp.dot(xb, w_ref[:, c0:c0 + n], preferred_element_type=F32)

    def rope(h):
        rot = jnp.where(first, pltpu.roll(h, LANES - HALF // 2, 1), pltpu.roll(h, HALF // 2, 1))
        return h * cos + rot * sin

    def blocks(h):
        return [h[:, k * LANES:(k + 1) * LANES] for k in range(h.shape[1] // LANES)]

    scale = A_HEAD_DIM ** -0.5 * math.log2(math.e)
    for k, h in enumerate(blocks(seg(_C_QA, 512))):
        qa_ref[:, k * LANES:(k + 1) * LANES] = (rope(h) * scale).astype(BF16)
    for k, h in enumerate(blocks(seg(_C_KA, 512))):
        r = rope(h)
        kaf_ref[:, k, :] = r
        kab_ref[:, k * LANES:(k + 1) * LANES] = r.astype(BF16)
    va = seg(_C_VA, 512)
    for k, h in enumerate(blocks(va)):
        vaf_ref[:, k, :] = h
    vab_ref[...] = va.astype(BF16)
    for k, h in enumerate(blocks(seg(_C_QB, 512))):
        qb_ref[:, k * LANES:(k + 1) * LANES] = (rope(h) * scale).astype(BF16)
    kbvb = seg(_C_KB, 256)
    kb = rope(kbvb[:, :LANES])
    kbb_ref[...] = kb.astype(BF16)
    vb = kbvb[:, LANES:]
    vbb_ref[...] = vb.astype(BF16)
    for ref, val in ((kbf_ref, kb.T), (vbf_ref, vb.T)):
        for g in range(B_KV_HEADS):
            ref[0, g] = val[g * B_HEAD_DIM:(g + 1) * B_HEAD_DIM]
    for k, h in enumerate(blocks(seg(_C_QI, 512))):
        qi_ref[:, k * LANES:(k + 1) * LANES] = rope(h).astype(BF16)
    kiwi = seg(_C_KIWI, LANES)
    ki = rope(kiwi)
    kif_ref[0] = ki.T[:IDX_DIM]
    kib_ref[...] = jnp.where(lane < HALF, ki, pltpu.roll(ki, HALF, 1)).astype(BF16)
    wi_ref[...] = pltpu.roll(kiwi, HALF, 1)[:, :IDX_HEADS]


def _project(x, w_pad, cos, sin, tm):
    n, d = x.shape
    t_tab = cos.shape[0]
    n_tab = t_tab // tm
    nseq = n // t_tab
    row = lambda i: (i, 0)
    row2 = lambda w, dt: (((n, w), dt), pl.BlockSpec((tm, w), row))
    diff_rows = (((n, A_HEADS, LANES), F32), pl.BlockSpec((tm, A_HEADS, LANES), lambda i: (i, 0, 0)))
    sparse_rows = (((nseq, B_KV_HEADS, B_HEAD_DIM, t_tab), F32),
                   pl.BlockSpec((1, B_KV_HEADS, B_HEAD_DIM, tm), lambda i: (i // n_tab, 0, 0, i % n_tab)))
    index_rows = (((nseq, IDX_DIM, t_tab), F32), pl.BlockSpec((1, IDX_DIM, tm), lambda i: (i // n_tab, 0, i % n_tab)))
    outs = [row2(512, BF16), diff_rows, row2(512, BF16), diff_rows, row2(512, BF16),
            row2(512, BF16), sparse_rows, row2(LANES, BF16), sparse_rows, row2(LANES, BF16),
            row2(512, BF16), index_rows, row2(LANES, BF16), row2(IDX_HEADS, F32)]
    return pl.pallas_call(
        _proj_kernel,
        grid=(n // tm,),
        in_specs=[pl.BlockSpec((tm, d), row),
                  pl.BlockSpec((d, _IN_COLS_PAD), lambda i: (0, 0)),
                  pl.BlockSpec((tm, LANES), lambda i: (i % n_tab, 0)),
                  pl.BlockSpec((tm, LANES), lambda i: (i % n_tab, 0))],
        out_specs=[spec for _, spec in outs],
        out_shape=[jax.ShapeDtypeStruct(s, dt) for (s, dt), _ in outs],
        compiler_params=_cparams(("parallel",)),
        name="proj",
    )(x, w_pad, cos, sin)


def _rope_tables(pos):
    inv = ROPE_THETA ** (-jnp.arange(0, HALF, 2, dtype=F32) / HALF)
    ang = pos.astype(F32)[:, None] * inv[None, :]
    cos = jnp.concatenate([jnp.cos(ang)] * 4, axis=-1)
    sin = jnp.concatenate([-jnp.sin(ang), jnp.sin(ang)] * 2, axis=-1)
    return cos, sin


def _diff_finish(acc, l, lam, g, lam_init, rows):
    o = acc[:rows] / l[:rows] - lam * (acc[rows:] / l[rows:])
    o = o * lax.rsqrt(jnp.mean(o * o, axis=-1, keepdims=True) + RMS_EPS)
    return o * g * (1.0 - lam_init)


ONES_ROWS = 16


def _with_ones_rows(vt):
    return jnp.concatenate([vt, jnp.ones(vt.shape[:-2] + (ONES_ROWS, vt.shape[-1]), vt.dtype)], axis=-2)


def _diff_kernel(lam_ref, q_ref, k_ref, vt_ref, g_ref, o_ref, m_scr, acc_scr, *, tq, lam_init):
    qi = pl.program_id(2)
    q = q_ref[0]
    lane = _lane_iota(q.shape)
    zero = jnp.zeros_like(q)
    qq = jnp.concatenate([jnp.where(lane < HALF, q, zero), jnp.where(lane >= HALF, q, zero)], axis=0)
    m_scr[...] = jnp.full(m_scr.shape, NEG_BIG, F32)
    acc_scr[...] = jnp.zeros(acc_scr.shape, F32)

    def step(j, diagonal):
        off = pl.multiple_of(j * tq, tq)
        s = _nt_dot(k_ref[0, pl.ds(off, tq), :], qq)
        if diagonal:
            kpos = lax.broadcasted_iota(I32, s.shape, 0)
            qpos = lax.broadcasted_iota(I32, s.shape, 1) % tq
            s = jnp.where(kpos <= qpos, s, NEG_BIG)
        m_old = m_scr[...]
        m_new = jnp.maximum(m_old, jnp.max(s, axis=0, keepdims=True))
        a = jnp.exp2(m_old - m_new)
        p = jnp.exp2(s - m_new)
        acc_scr[...] = a * acc_scr[...] + jnp.dot(vt_ref[0, 0, j], p.astype(BF16), preferred_element_type=F32)
        m_scr[...] = m_new

    def body(j, carry):
        step(j, False)
        return carry

    lax.fori_loop(0, qi, body, 0)
    step(qi, True)
    o = acc_scr[:LANES, :] * (1.0 / acc_scr[LANES:LANES + 1, :])
    o = o[:, :tq] - lam_ref[0, 0] * o[:, tq:]
    o = o * lax.rsqrt(jnp.mean(o * o, axis=0, keepdims=True) + RMS_EPS)
    o = o * g_ref[...] * (1.0 - lam_init)
    o_ref[0] = o.T.astype(o_ref.dtype)


def _diff_attention(lam, qa, ka, va, g, lam_init, tq):
    b, t, _ = qa.shape
    nblk = t // tq
    vt = _with_ones_rows(jnp.transpose(va.reshape(b, nblk, tq, A_HEADS, LANES), (0, 3, 1, 4, 2)))
    rows = LANES + ONES_ROWS
    return pl.pallas_call(
        functools.partial(_diff_kernel, tq=tq, lam_init=lam_init),
        grid=(b, A_HEADS, nblk),
        in_specs=[pl.BlockSpec(memory_space=pltpu.SMEM),
                  pl.BlockSpec((1, tq, LANES), lambda b_, h, i: (b_, i, h)),
                  pl.BlockSpec((1, t, LANES), lambda b_, h, i: (b_, 0, h)),
                  pl.BlockSpec((1, 1, nblk, rows, tq), lambda b_, h, i: (b_, h, 0, 0, 0)),
                  pl.BlockSpec((LANES, 1), lambda b_, h, i: (0, 0))],
        out_specs=pl.BlockSpec((1, tq, LANES), lambda b_, h, i: (b_, i, h)),
        out_shape=jax.ShapeDtypeStruct((b, t, A_HEADS * LANES), BF16),
        scratch_shapes=[pltpu.VMEM((1, 2 * tq), F32), pltpu.VMEM((rows, 2 * tq), F32)],
        compiler_params=_cparams(("parallel", "parallel", "arbitrary")),
        name="diff_attn",
    )(lam, qa, ka, vt, g.reshape(LANES, 1))


def _float_key(x):
    b = lax.bitcast_convert_type(x + 0.0, I32)
    return b ^ ((b >> 31) & 0x7FFFFFFF)


I16 = jnp.int16
I16_MIN = -(2 ** 15)


def _split_key(key):
    return (key >> 16).astype(I16), ((key & 0xFFFF) - 2 ** 15).astype(I16)


def _select_topk(key_scr, hi_scr, lo_scr, nch, ck, topk):
    nq = key_scr.shape[1]
    kf = float(topk)
    grp = 64
    assert key_scr.shape[0] // grp < 2 ** 15

    def counts(plane, thr, strict):
        thr_b = jnp.broadcast_to(thr.astype(plane.dtype), (grp, nq))
        one, zero = jnp.ones((), plane.dtype), jnp.zeros((), plane.dtype)

        def body(c, cnt):
            off = pl.multiple_of(c * ck, ck)
            for g in range(ck // grp):
                kk = plane[pl.ds(off + g * grp, grp), :]
                hit = (kk > thr_b) if strict else (kk >= thr_b)
                cnt = cnt + jnp.where(hit, one, zero)
            return cnt
        cnt = lax.fori_loop(0, nch, body, jnp.zeros((grp, nq), plane.dtype))
        return jnp.sum(cnt.astype(F32), axis=0, keepdims=True)

    def search16(plane, base):
        tau = jnp.where(base + counts(plane, jnp.zeros((1, nq), I32), False) >= kf, 0, I16_MIN).astype(I32)

        def bit_body(i, tau):
            cand = tau | jnp.left_shift(jnp.int32(1), 14 - i)
            return jnp.where(base + counts(plane, cand, False) >= kf, cand, tau)

        return lax.fori_loop(0, 15, bit_body, tau)

    tau_hi = search16(hi_scr, jnp.zeros((1, nq), F32))
    above = counts(hi_scr, tau_hi, True)
    hi_b = jnp.broadcast_to(tau_hi.astype(I16), (grp, nq))

    def mask_body(c, carry):
        off = pl.multiple_of(c * ck, ck)
        for g in range(ck // grp):
            sl = pl.ds(off + g * grp, grp)
            lo_scr[sl, :] = jnp.where(hi_scr[sl, :] == hi_b, lo_scr[sl, :], jnp.full((), I16_MIN, I16))
        return carry

    lax.fori_loop(0, nch, mask_body, 0)
    tau_lo = search16(lo_scr, above)
    tau = tau_hi * 2 ** 16 + (tau_lo + 2 ** 15)

    def bcast(v, rows=grp):
        return jnp.broadcast_to(v, (rows, nq))

    cnt_ge = counts(key_scr, tau, False)
    cnt_gt = counts(key_scr, tau, True)
    overfull = jnp.where((cnt_ge > kf) & (tau > INT_MIN), 1.0, 0.0)

    @pl.when(jnp.max(overfull) > 0.0)
    def _():
        tau_b = bcast(tau, LANES)
        need_b = bcast(kf - cnt_gt, LANES)
        lower = jnp.where(lax.broadcasted_iota(I32, (LANES, LANES), 0)
                          >= lax.broadcasted_iota(I32, (LANES, LANES), 1), 1.0, 0.0).astype(BF16)

        def body(c, run):
            off = pl.multiple_of(c * ck, ck)
            for g in range(ck // LANES):
                sl = pl.ds(off + g * LANES, LANES)
                kk = key_scr[sl, :]
                eq = kk == tau_b
                eqf = jnp.where(eq, 1.0, 0.0)
                pre = jnp.dot(lower, eqf.astype(BF16), preferred_element_type=F32)
                drop = eq & ((pre + run) > need_b)
                key_scr[sl, :] = jnp.where(drop, INT_MIN, kk)
                run = run + jnp.sum(eqf, axis=0, keepdims=True)
            return run

        lax.fori_loop(0, nch, body, jnp.zeros((1, nq), F32))

    return tau


def _stack_heads(q, n_heads):
    lane = _lane_iota((q.shape[0], LANES))
    zero = jnp.zeros((q.shape[0], LANES), q.dtype)
    parts = []
    for h in range(n_heads):
        blk = q[:, (h // 2) * LANES:(h // 2 + 1) * LANES]
        parts.append(jnp.where((lane < HALF) == (h % 2 == 0), blk, zero))
    return jnp.concatenate(parts, axis=0)


M_INIT = -1e29


def _sparse_kernel(qi_ref, wt_ref, ki_ref, qb_ref, kb_ref, vt_ref, o_ref, key_scr, hi_scr, lo_scr, m_scr, acc_scr,
                   *, tq, ck, topk):
    i = pl.program_id(1)
    nch = ((i + 1) * tq + ck - 1) // ck
    kpos0 = lax.broadcasted_iota(I32, (ck, tq), 0)
    qpos = i * tq + lax.broadcasted_iota(I32, (ck, tq), 1)

    qs = _stack_heads(qi_ref[0], IDX_HEADS)
    wt = wt_ref[0]

    def score_body(c, carry):
        off = pl.multiple_of(c * ck, ck)
        d = _nt_dot(ki_ref[0, pl.ds(off, ck), :], qs)
        sc = jnp.zeros((ck, tq), F32)
        for h in range(IDX_HEADS):
            sc = sc + jnp.maximum(d[:, h * tq:(h + 1) * tq], 0.0) * wt[h:h + 1, :]
        sc = sc * IDX_SCALE
        key = jnp.where(kpos0 + off <= qpos, _float_key(sc), INT_MIN)
        key_scr[pl.ds(off, ck), :] = key
        hi_scr[pl.ds(off, ck), :], lo_scr[pl.ds(off, ck), :] = _split_key(key)
        return carry

    lax.fori_loop(0, nch, score_body, 0)
    tau = _select_topk(key_scr, hi_scr, lo_scr, nch, ck, topk)
    tau_b = jnp.broadcast_to(tau, (ck, tq))

    qh = []
    qb = qb_ref[0]
    lane = _lane_iota((tq, LANES))
    zero = jnp.zeros((tq, LANES), qb.dtype)
    grp = B_HEADS // B_KV_HEADS
    for h in range(B_HEADS):
        blk = qb[:, (h // 2) * LANES:(h // 2 + 1) * LANES]
        if h % 2 != h // grp:
            blk = pltpu.roll(blk, HALF, 1)
        qh.append(jnp.where((lane >= HALF) == (h // grp == 1), blk, zero))
    qs2 = jnp.concatenate(qh, axis=0)
    m_scr[...] = jnp.full(m_scr.shape, M_INIT, F32)
    acc_scr[...] = jnp.zeros(acc_scr.shape, F32)

    def attn_body(c, carry):
        off = pl.multiple_of(c * ck, ck)
        sel = (key_scr[pl.ds(off, ck), :] >= tau_b) & (kpos0 + off <= qpos)
        bias = jnp.where(sel, 0.0, NEG_BIG)
        s = _nt_dot(kb_ref[0, pl.ds(off, ck), :], qs2)
        ps, scales = [], []
        for h in range(B_HEADS):
            hs = slice(h * tq, (h + 1) * tq)
            sh = s[:, hs] + bias
            m_old = m_scr[:, hs]
            m_new = jnp.maximum(m_old, jnp.max(sh, axis=0, keepdims=True))
            a = jnp.exp2(m_old - m_new)
            p = jnp.exp2(sh - m_new)
            m_scr[:, hs] = m_new
            ps.append(p.astype(BF16))
            scales.append(a)
        pv = jnp.dot(vt_ref[0, c], jnp.concatenate(ps, axis=1), preferred_element_type=F32)
        acc_scr[...] = jnp.concatenate(scales, axis=1) * acc_scr[...] + pv
        return carry

    lax.fori_loop(0, nch, attn_body, 0)
    o = acc_scr[:LANES, :] * (1.0 / acc_scr[LANES:LANES + 1, :])
    parts = [o[(h // grp) * HALF:(h // grp + 1) * HALF, h * tq:(h + 1) * tq] for h in range(B_HEADS)]
    o_ref[0] = jnp.concatenate(parts, axis=0).T.astype(o_ref.dtype)


def _sparse_attention(qi, wi, ki2, qb, kb, vb, tq, ck):
    b, t, _ = qi.shape
    topk = min(TOPK_MAX, t // 4)
    wt = jnp.transpose(wi, (0, 2, 1))
    vt = _with_ones_rows(jnp.transpose(vb.reshape(b, t // ck, ck, LANES), (0, 1, 3, 2)))
    rows = LANES + ONES_ROWS
    blk_q = lambda w: pl.BlockSpec((1, tq, w), lambda b_, i: (b_, i, 0))
    blk_k = pl.BlockSpec((1, t, LANES), lambda b_, i: (b_, 0, 0))
    return pl.pallas_call(
        functools.partial(_sparse_kernel, tq=tq, ck=ck, topk=topk),
        grid=(b, t // tq),
        in_specs=[blk_q(512), pl.BlockSpec((1, IDX_HEADS, tq), lambda b_, i: (b_, 0, i)), blk_k, blk_q(512), blk_k,
                  pl.BlockSpec((1, t // ck, rows, ck), lambda b_, i: (b_, 0, 0, 0))],
        out_specs=blk_q(512),
        out_shape=jax.ShapeDtypeStruct((b, t, 512), BF16),
        scratch_shapes=[pltpu.VMEM((t, tq), I32), pltpu.VMEM((t, tq), I16), pltpu.VMEM((t, tq), I16),
                        pltpu.VMEM((1, B_HEADS * tq), F32),
                        pltpu.VMEM((rows, B_HEADS * tq), F32)],
        compiler_params=_cparams(("parallel", "arbitrary")),
        name="sparse_attn",
    )(qi, wt, ki2, qb, kb, vt)


def _page_specs(n, block, pg):
    zeros = (0,) * (len(block) - 1)
    return [pl.BlockSpec(block, functools.partial(lambda i, b, j, pt: (pt[b, j * pg + i],) + zeros, i))
            for i in range(n)]


def _dec_diff_kernel(pt_ref, lam_ref, q_ref, kn_ref, vn_ref, g_ref, qi_ref, w_ref, kin_ref, *rest, pg, lam_init):
    k_refs, v_refs, ki_refs = rest[:pg], rest[pg:2 * pg], rest[2 * pg:3 * pg]
    o_ref, sc_ref, new_ref, m_scr, l_scr, acc_scr, kbuf, vbuf, kibuf = rest[3 * pg:]
    j = pl.program_id(1)

    @pl.when(j == 0)
    def _():
        m_scr[...] = jnp.full(m_scr.shape, NEG_BIG, F32)
        l_scr[...] = jnp.zeros(l_scr.shape, F32)
        acc_scr[...] = jnp.zeros(acc_scr.shape, F32)

    rows_pg = k_refs[0].shape[1]
    page = ki_refs[0].shape[2]
    for i in range(pg):
        kbuf[i * rows_pg:(i + 1) * rows_pg, :] = k_refs[i][0].astype(BF16)
        vbuf[i * rows_pg:(i + 1) * rows_pg, :] = v_refs[i][0].astype(BF16)
        kibuf[:, i * page:(i + 1) * page] = ki_refs[i][0].astype(BF16)

    q = q_ref[0]
    s = _nt_dot(q, kbuf[...])
    same_head = (_lane_iota(s.shape) % A_HEADS) == (lax.broadcasted_iota(I32, s.shape, 0) % A_HEADS)
    s = jnp.where(same_head, s, NEG_BIG)
    m_old = m_scr[...]
    m_new = jnp.maximum(m_old, jnp.max(s, axis=1, keepdims=True))
    a = jnp.exp2(m_old - m_new)
    p = jnp.exp2(s - m_new).astype(BF16)
    l_scr[...] = a * l_scr[...] + jnp.sum(p.astype(F32), axis=1, keepdims=True)
    acc_scr[...] = a * acc_scr[...] + jnp.dot(p, vbuf[...], preferred_element_type=F32)
    m_scr[...] = m_new

    qi = qi_ref[0]
    w = w_ref[0]
    d = jnp.dot(qi, kibuf[...], preferred_element_type=F32)
    sc_ref[0] = jnp.sum(jnp.maximum(d, 0.0) * w, axis=0, keepdims=True) * IDX_SCALE

    @pl.when(j == pl.num_programs(1) - 1)
    def _():
        s_new = jnp.sum(q.astype(F32) * kn_ref[0], axis=1, keepdims=True)
        m_old = m_scr[...]
        m_new = jnp.maximum(m_old, s_new)
        a = jnp.exp2(m_old - m_new)
        pn = jnp.exp2(s_new - m_new)
        l = a * l_scr[...] + pn
        acc = a * acc_scr[...] + pn * vn_ref[0]
        o = _diff_finish(acc, l, lam_ref[0, 0], g_ref[...], lam_init, A_HEADS)
        o_ref[0] = jnp.concatenate([o, jnp.zeros_like(o)], axis=0)
        dn = jnp.sum(qi.astype(F32) * kin_ref[0], axis=1, keepdims=True)
        scn = jnp.sum(jnp.maximum(dn, 0.0) * w, axis=0, keepdims=True) * IDX_SCALE
        new_ref[0] = jnp.broadcast_to(scn, new_ref.shape[1:])


def _dec_diff_attention(pt, lam, qm, knew, vnew, g, qi, wi, kinew, kpages, vpages, kipages, lam_init, pg):
    db, npg = pt.shape
    rows_pg = kpages.shape[1]
    page = kipages.shape[2]
    per_b = lambda shape: pl.BlockSpec(shape, lambda b, j, pt_: (b,) + (0,) * (len(shape) - 1))
    return pl.pallas_call(
        functools.partial(_dec_diff_kernel, pg=pg, lam_init=lam_init),
        grid_spec=pltpu.PrefetchScalarGridSpec(
            num_scalar_prefetch=1,
            grid=(db, npg // pg),
            in_specs=[pl.BlockSpec(memory_space=pltpu.SMEM), per_b((1, 8, LANES)), per_b((1, 8, LANES)),
                      per_b((1, 8, LANES)), pl.BlockSpec((1, LANES), lambda b, j, pt_: (0, 0)),
                      per_b((1, IDX_HEADS, IDX_DIM)), per_b((1, IDX_HEADS, 1)), per_b((1, 1, IDX_DIM))]
                     + _page_specs(pg, (1,) + kpages.shape[1:], pg) + _page_specs(pg, (1,) + vpages.shape[1:], pg)
                     + _page_specs(pg, (1,) + kipages.shape[1:], pg),
            out_specs=[per_b((1, 8, LANES)), pl.BlockSpec((1, 1, pg * page), lambda b, j, pt_: (b, 0, j)),
                       per_b((1, 1, LANES))],
            scratch_shapes=[pltpu.VMEM((8, 1), F32), pltpu.VMEM((8, 1), F32), pltpu.VMEM((8, LANES), F32),
                            pltpu.VMEM((pg * rows_pg, LANES), BF16), pltpu.VMEM((pg * rows_pg, LANES), BF16),
                            pltpu.VMEM((IDX_DIM, pg * page), BF16)],
        ),
        out_shape=[jax.ShapeDtypeStruct((db, 8, LANES), F32), jax.ShapeDtypeStruct((db, 1, npg * page), F32),
                   jax.ShapeDtypeStruct((db, 1, LANES), F32)],
        compiler_params=_cparams(("parallel", "arbitrary")),
        name="dec_diff_attn",
    )(pt, lam, qm, knew, vnew, g, qi, wi, kinew, *([kpages] * pg), *([vpages] * pg), *([kipages] * pg))


def _dec_select_kernel(sc_ref, key_ref, tau_ref, hi_scr, lo_scr, *, n_keys, topk):
    width, nq = sc_ref.shape
    kpos = lax.broadcasted_iota(I32, (width, nq), 0)
    key = jnp.where(kpos < n_keys, _float_key(sc_ref[...]), INT_MIN)
    key_ref[...] = key
    hi_scr[...], lo_scr[...] = _split_key(key)
    tau = _select_topk(key_ref, hi_scr, lo_scr, width // LANES, LANES, topk)
    tau_ref[...] = jnp.broadcast_to(tau, tau_ref.shape)


def _dec_select(scores_t, n_keys, topk):
    width, nq = scores_t.shape
    return pl.pallas_call(
        functools.partial(_dec_select_kernel, n_keys=n_keys, topk=topk),
        out_shape=[jax.ShapeDtypeStruct((width, nq), I32), jax.ShapeDtypeStruct((8, nq), I32)],
        scratch_shapes=[pltpu.VMEM((width, nq), I16), pltpu.VMEM((width, nq), I16)],
        compiler_params=pltpu.CompilerParams(vmem_limit_bytes=VMEM_LIMIT),
        name="dec_select",
    )(scores_t)


def _dec_sparse_kernel(pt_ref, q_ref, key_ref, keyn_ref, tau_ref, kn_ref, vn_ref, *rest, pg):
    k_refs, v_refs = rest[:pg], rest[pg:2 * pg]
    o_ref, m_scr, l_scr, acc_scr, kbuf, vbuf = rest[2 * pg:]
    j = pl.program_id(1)

    @pl.when(j == 0)
    def _():
        m_scr[...] = jnp.full(m_scr.shape, NEG_BIG, F32)
        l_scr[...] = jnp.zeros(l_scr.shape, F32)
        acc_scr[...] = jnp.zeros(acc_scr.shape, F32)

    page = k_refs[0].shape[3]
    for i in range(pg):
        kbuf[:, :, i * page:(i + 1) * page] = k_refs[i][0].astype(BF16)
        vbuf[:, :, i * page:(i + 1) * page] = v_refs[i][0].astype(BF16)
    tau = tau_ref[0][:, 0:1]
    sel = jnp.broadcast_to(key_ref[0] >= tau, (8, pg * page))
    for g in range(B_KV_HEADS):
        q = q_ref[0, g]
        s = jnp.dot(q, kbuf[g], preferred_element_type=F32)
        s = jnp.where(sel, s, NEG_BIG)
        m_old = m_scr[g]
        m_new = jnp.maximum(m_old, jnp.max(s, axis=1, keepdims=True))
        a = jnp.exp2(m_old - m_new)
        p = jnp.where(sel, jnp.exp2(s - m_new), 0.0).astype(BF16)
        l_scr[g] = a * l_scr[g] + jnp.sum(p.astype(F32), axis=1, keepdims=True)
        acc_scr[g] = a * acc_scr[g] + _nt_dot(p, vbuf[g])
        m_scr[g] = m_new

    @pl.when(j == pl.num_programs(1) - 1)
    def _():
        sel_new = keyn_ref[0][:, 0:1] >= tau
        for g in range(B_KV_HEADS):
            s_new = jnp.sum(q_ref[0, g].astype(F32) * kn_ref[0, g], axis=1, keepdims=True)
            s_new = jnp.where(sel_new, s_new, NEG_BIG)
            m_old = m_scr[g]
            m_new = jnp.maximum(m_old, s_new)
            a = jnp.exp2(m_old - m_new)
            pn = jnp.where(sel_new, jnp.exp2(s_new - m_new), 0.0)
            l = a * l_scr[g] + pn
            o_ref[0, g] = (a * acc_scr[g] + pn * vn_ref[0, g]) / l


def _dec_sparse_attention(pt, q, keys, tau, kbnew, vbnew, kpages, vpages, pg):
    db, npg = pt.shape
    page = kpages.shape[3]
    per_b = lambda shape: pl.BlockSpec(shape, lambda b, j, pt_: (b,) + (0,) * (len(shape) - 1))
    blk = (1,) + kpages.shape[1:]
    return pl.pallas_call(
        functools.partial(_dec_sparse_kernel, pg=pg),
        grid_spec=pltpu.PrefetchScalarGridSpec(
            num_scalar_prefetch=1,
            grid=(db, npg // pg),
            in_specs=[per_b((1, B_KV_HEADS, 8, B_HEAD_DIM)),
                      pl.BlockSpec((1, 1, pg * page), lambda b, j, pt_: (b, 0, j)),
                      pl.BlockSpec((1, 1, LANES), lambda b, j, pt_: (b, 0, npg * page // LANES)),
                      per_b((1, 1, LANES)), per_b((1, B_KV_HEADS, 1, B_HEAD_DIM)),
                      per_b((1, B_KV_HEADS, 1, B_HEAD_DIM))]
                     + _page_specs(pg, blk, pg) + _page_specs(pg, blk, pg),
            out_specs=per_b((1, B_KV_HEADS, 8, B_HEAD_DIM)),
            scratch_shapes=[pltpu.VMEM((B_KV_HEADS, 8, 1), F32), pltpu.VMEM((B_KV_HEADS, 8, 1), F32),
                            pltpu.VMEM((B_KV_HEADS, 8, B_HEAD_DIM), F32),
                            pltpu.VMEM((B_KV_HEADS, B_HEAD_DIM, pg * page), BF16),
                            pltpu.VMEM((B_KV_HEADS, B_HEAD_DIM, pg * page), BF16)],
        ),
        out_shape=jax.ShapeDtypeStruct((db, B_KV_HEADS, 8, B_HEAD_DIM), F32),
        compiler_params=_cparams(("parallel", "arbitrary")),
        name="dec_sparse_attn",
    )(pt, q, keys, keys, tau, kbnew, vbnew, *([kpages] * pg), *([vpages] * pg))


def _layer_norm(x, g, b):
    mu = jnp.mean(x, axis=-1, keepdims=True)
    xc = x - mu
    var = jnp.mean(xc * xc, axis=-1, keepdims=True)
    return xc * lax.rsqrt(var + LN_EPS) * g + b


def _post_kernel(x_ref, ma_ref, mb_ref, woa_ref, wob_ref, g_ref, b_ref, rw_ref, rb_ref, h_ref, c_ref):
    y = (DEEPNORM_ALPHA * x_ref[...]
         + jnp.dot(ma_ref[...], woa_ref[...], preferred_element_type=F32)
         + jnp.dot(mb_ref[...], wob_ref[...], preferred_element_type=F32))
    h = _layer_norm(y, g_ref[...], b_ref[...])
    h_ref[...] = h
    logits = jnp.dot(h, rw_ref[...], preferred_element_type=F32, precision=lax.Precision.HIGHEST) + rb_ref[...]
    lane = _lane_iota(logits.shape)
    work = logits
    sel = jnp.zeros(logits.shape, jnp.bool_)
    for _ in range(TOP_K):
        mx = jnp.max(work, axis=-1, keepdims=True)
        idx = jnp.min(jnp.where(work == mx, lane, N_EXPERTS), axis=-1, keepdims=True)
        pick = lane == idx
        sel = sel | pick
        work = jnp.where(pick, -jnp.inf, work)
    e = jnp.where(sel, jnp.exp(logits - jnp.max(logits, axis=-1, keepdims=True)), 0.0)
    c_ref[...] = e / jnp.sum(e, axis=-1, keepdims=True)


def _post_mixer(x, mix_a, mix_b, wo_a, wo_b, g, b, rw, rb, tm):
    n, d = x.shape
    row = lambda w: pl.BlockSpec((tm, w), lambda i: (i, 0))
    full = lambda a: pl.BlockSpec(a.shape, lambda i: (0, 0))
    return pl.pallas_call(
        _post_kernel,
        grid=(n // tm,),
        in_specs=[row(d), row(mix_a.shape[1]), row(mix_b.shape[1]), full(wo_a), full(wo_b), full(g), full(b),
                  full(rw), full(rb)],
        out_specs=[row(d), row(N_EXPERTS)],
        out_shape=[jax.ShapeDtypeStruct((n, d), F32), jax.ShapeDtypeStruct((n, N_EXPERTS), F32)],
        compiler_params=_cparams(("parallel",)),
        name="post_mixer",
    )(x, mix_a, mix_b, wo_a, wo_b, g, b, rw, rb)


def _moe_kernel(cnt_ref, h_ref, comb_ref, upper_ref, wg_ref, bg_ref, wu_ref, bu_ref, wd_ref, bd_ref,
                g_ref, b_ref, o_ref, xs_scr, pm_scr, gs_scr, ys_scr, *, tt, cs, nsub):
    i = pl.program_id(0)
    e = pl.program_id(1)
    n_exp = pl.num_programs(1)

    @pl.when(e == 0)
    def _():
        o_ref[...] = jnp.zeros(o_ref.shape, F32)

    slot0 = lax.broadcasted_iota(I32, (cs, tt), 0).astype(F32) + 1.0

    def onehot_of(j, first):
        gate = comb_ref[0, 0, pl.ds(j, 1), :]
        sel = gate > 0.0
        self_ = jnp.broadcast_to(jnp.where(sel, 1.0, 0.0), (8, tt)).astype(BF16)
        rank = jnp.dot(self_, upper_ref[...], preferred_element_type=F32)[0:1]
        onehot = jnp.broadcast_to(sel, (cs, tt)) & (jnp.broadcast_to(rank, (cs, tt)) == slot0 + first)
        return onehot, jnp.broadcast_to(gate, (cs, tt))

    def expert(xs):
        gg = jnp.dot(xs, wg_ref[0], preferred_element_type=F32) + bg_ref[0]
        uu = jnp.dot(xs, wu_ref[0], preferred_element_type=F32) + bu_ref[0]
        gg = jnp.minimum(gg, SWIGLU_LIMIT)
        uu = jnp.clip(uu, -SWIGLU_LIMIT, SWIGLU_LIMIT)
        hh = gg * jax.nn.sigmoid(SWIGLU_ALPHA * gg) * (uu + 1.0)
        return jnp.dot(hh.astype(BF16), wd_ref[0], preferred_element_type=F32) + bd_ref[0]

    for j in range(nsub):
        onehot, gate_b = onehot_of(j, 0.0)
        pm = jnp.where(onehot, 1.0, 0.0).astype(BF16)
        pm_scr[j] = pm
        xs_scr[j * cs:(j + 1) * cs, :] = jnp.dot(pm, h_ref[j * tt:(j + 1) * tt, :].astype(BF16),
                                                 preferred_element_type=F32).astype(BF16)
        gs_scr[j * cs:(j + 1) * cs, :] = jnp.sum(jnp.where(onehot, gate_b, 0.0), axis=1, keepdims=True)
    ys_scr[...] = (expert(xs_scr[...]) * gs_scr[...]).astype(BF16)
    for j in range(nsub):
        o_ref[j * tt:(j + 1) * tt, :] += _tn_dot(pm_scr[j], ys_scr[j * cs:(j + 1) * cs, :])

    def sub_body(j, carry):
        n = cnt_ref[(e * pl.num_programs(0) + i) * nsub + j]

        @pl.when(n > cs)
        def _():
            off = pl.multiple_of(j * tt, tt)
            xj = h_ref[pl.ds(off, tt), :].astype(BF16)

            def chunk_body(c, carry2):
                onehot, gate_b = onehot_of(j, (c * cs).astype(F32))
                pm = jnp.where(onehot, 1.0, 0.0).astype(BF16)
                y = expert(jnp.dot(pm, xj, preferred_element_type=F32).astype(BF16))
                gs = jnp.sum(jnp.where(onehot, gate_b, 0.0), axis=1, keepdims=True)
                o_ref[pl.ds(off, tt), :] += _tn_dot(pm, (y * gs).astype(BF16))
                return carry2

            lax.fori_loop(1, (n + cs - 1) // cs, chunk_body, 0)

        return carry

    lax.fori_loop(0, nsub, sub_body, 0)

    @pl.when(e == n_exp - 1)
    def _():
        o_ref[...] = _layer_norm(DEEPNORM_ALPHA * h_ref[...] + o_ref[...], g_ref[...], b_ref[...])


def _moe(h, comb, wg, bg, wu, bu, wd, bd, g, b, tb, tt, cs):
    n, d = h.shape
    n_exp, _, f = wg.shape
    nsub = tb // tt
    nblk = n // tb
    comb_t = comb.T
    counts = jnp.sum((comb_t > 0.0).reshape(n_exp * nblk * nsub, tt), axis=-1).astype(I32)
    comb_t = comb_t.reshape(n_exp, nblk, nsub, tt)
    upper = (jnp.arange(tt)[:, None] <= jnp.arange(tt)[None, :]).astype(BF16)
    wspec = lambda a, b_: pl.BlockSpec((1, a, b_), lambda i, e, c: (e, 0, 0))
    vec = pl.BlockSpec((1, d), lambda i, e, c: (0, 0))
    return pl.pallas_call(
        functools.partial(_moe_kernel, tt=tt, cs=cs, nsub=nsub),
        grid_spec=pltpu.PrefetchScalarGridSpec(
            num_scalar_prefetch=1,
            grid=(nblk, n_exp),
            in_specs=[pl.BlockSpec((tb, d), lambda i, e, c: (i, 0), pipeline_mode=pl.Buffered(1)),
                      pl.BlockSpec((1, 1, nsub, tt), lambda i, e, c: (e, i, 0, 0)),
                      pl.BlockSpec((tt, tt), lambda i, e, c: (0, 0)),
                      wspec(d, f), wspec(1, f), wspec(d, f), wspec(1, f), wspec(f, d), wspec(1, d), vec, vec],
            out_specs=pl.BlockSpec((tb, d), lambda i, e, c: (i, 0)),
            scratch_shapes=[pltpu.VMEM((nsub * cs, d), BF16), pltpu.VMEM((nsub, cs, tt), BF16),
                            pltpu.VMEM((nsub * cs, 1), F32), pltpu.VMEM((nsub * cs, d), BF16)],
        ),
        out_shape=jax.ShapeDtypeStruct((n, d), F32),
        compiler_params=_cparams(("parallel", "arbitrary")),
        name="moe",
    )(counts, h, comb_t, upper, wg, bg.reshape(n_exp, 1, f), wu, bu.reshape(n_exp, 1, f),
      wd, bd.reshape(n_exp, 1, d), g, b)


def _pad_w_in(w_in):
    d = w_in.shape[0]
    return jnp.concatenate([w_in, jnp.zeros((d, _IN_COLS_PAD - _IN_COLS), w_in.dtype)], axis=1).astype(BF16)


def _prep_params(p, l=0):
    half = p["w_out"].shape[1] // 2
    vec = lambda a: a[l].reshape(1, -1)
    return {
        "wo_a": p["w_out"][l, :half].astype(BF16), "wo_b": p["w_out"][l, half:].astype(BF16),
        "ln1_g": vec(p["ln1_g"]), "ln1_b": vec(p["ln1_b"]), "ln2_g": vec(p["ln2_g"]), "ln2_b": vec(p["ln2_b"]),
        "router_w": p["router_w"][l], "router_b": vec(p["router_b"]),
        "w_gate": p["w_gate"][l].astype(BF16), "b_gate": p["b_gate"][l],
        "w_up": p["w_up"][l].astype(BF16), "b_up": p["b_up"][l],
        "w_down": p["w_down"][l].astype(BF16), "b_down": p["b_down"][l],
    }


def _pick_tile(n, pref):
    t = min(n, pref)
    while n % t:
        t //= 2
    return t


def _token_major(a):
    a = jnp.moveaxis(a, -1, 1)
    return a.reshape((a.shape[0] * a.shape[1],) + a.shape[2:])


def _cache_rows(kaf, vaf, kbf, vbf, kif, b, t):
    return (kaf.reshape(1, b, t, A_HEADS, 2 * A_HEAD_DIM), vaf.reshape(1, b, t, A_HEADS, 2 * A_HEAD_DIM),
            _token_major(kbf).reshape(1, b, t, B_KV_HEADS, B_HEAD_DIM),
            _token_major(vbf).reshape(1, b, t, B_KV_HEADS, B_HEAD_DIM),
            _token_major(kif).reshape(1, b, t, IDX_DIM))


def _prompt_mixers(x_prompt, w_pad, lam, subln_g, lam_init):
    b, t, d = x_prompt.shape
    n = b * t
    cos, sin = _rope_tables(jnp.arange(t))
    (qa, kaf, kab, vaf, vab, qb, kbf, kbb, vbf, vbb, qi, kif, kib, wi) = _project(
        x_prompt.reshape(n, d), w_pad, cos, sin, _pick_tile(t, 512))
    r3 = lambda a: a.reshape(b, t, a.shape[-1])
    mix_a = _diff_attention(lam, r3(qa), r3(kab), r3(vab), subln_g, lam_init, _pick_tile(t, 1024))
    mix_b = _sparse_attention(r3(qi), r3(wi), r3(kib), r3(qb), r3(kbb), r3(vbb), _pick_tile(t, 512),
                              _pick_tile(t, 512))
    return mix_a.reshape(n, -1), mix_b.reshape(n, -1), _cache_rows(kaf, vaf, kbf, vbf, kif, b, t)


def _decode_mixers(x_sample, w_pad, lam, subln_g, lam_init, caches, page_table, pg):
    ck_diff, cv_diff, ck_sparse, cv_sparse, ck_index = caches
    db, _, d = x_sample.shape
    n_pool, page = ck_diff.shape[:2]
    npg = page_table.shape[1]
    past = npg * page
    cos, sin = _rope_tables(jnp.full((db,), past))
    (qa, kaf, _, vaf, _, qb, kbf, _, vbf, _, qi, kif, _, wi) = _project(x_sample.reshape(db, d), w_pad, cos, sin, db)

    qa3 = qa.reshape(db, A_HEADS, LANES)
    lane = _lane_iota(qa3.shape[1:])[None]
    zero = jnp.zeros_like(qa3)
    qm = jnp.concatenate([jnp.where(lane < HALF, qa3, zero), jnp.where(lane >= HALF, qa3, zero)], axis=1)
    twice = lambda a: jnp.concatenate([a.reshape(db, A_HEADS, LANES)] * 2, axis=1)
    out_a, scores, sc_new = _dec_diff_attention(
        page_table, lam, qm, twice(kaf), twice(vaf), subln_g,
        qi.reshape(db, IDX_HEADS, IDX_DIM), wi.reshape(db, IDX_HEADS, 1), _token_major(kif).reshape(db, 1, IDX_DIM),
        ck_diff.reshape(n_pool, page * A_HEADS, LANES), cv_diff.reshape(n_pool, page * A_HEADS, LANES),
        jnp.transpose(ck_index, (0, 2, 1)), lam_init, pg)
    mix_a = out_a[:, :A_HEADS].reshape(db, A_HEADS * LANES).astype(BF16)
    sc_all = jnp.concatenate([scores[:, 0], sc_new[:, 0, :1], jnp.zeros((db, LANES - 1), F32)], axis=1)
    keys_t, tau = _dec_select(sc_all.T, past + 1, min(TOPK_MAX, (past + 1) // 4))
    keys = keys_t.T
    tau = jnp.broadcast_to(tau[0].reshape(db, 1, 1), (db, 1, LANES))

    grp = B_HEADS // B_KV_HEADS
    qb4 = qb.reshape(db, B_KV_HEADS, grp, B_HEAD_DIM)
    qs = jnp.concatenate([qb4, jnp.zeros((db, B_KV_HEADS, 8 - grp, B_HEAD_DIM), qb.dtype)], axis=2)
    out_b = _dec_sparse_attention(page_table, qs, keys.reshape(db, 1, -1), tau,
                                  _token_major(kbf).reshape(db, B_KV_HEADS, 1, B_HEAD_DIM),
                                  _token_major(vbf).reshape(db, B_KV_HEADS, 1, B_HEAD_DIM),
                                  jnp.transpose(ck_sparse, (0, 2, 3, 1)), jnp.transpose(cv_sparse, (0, 2, 3, 1)),
                                  _pick_tile(npg, 2 * pg))
    mix_b = out_b[:, :, :grp].reshape(db, B_HEADS * B_HEAD_DIM).astype(BF16)
    return mix_a, mix_b, _cache_rows(kaf, vaf, kbf, vbf, kif, db, 1)


def _slot_rows(tt):
    pack = 16
    want = tt * TOP_K // N_EXPERTS
    want += want // 2
    return max(pack, -(-want // pack) * pack)


def _post_layer(x, mix_a, mix_b, p, tb, tt):
    n, d = x.shape
    h, comb = _post_mixer(x, mix_a, mix_b, p["wo_a"], p["wo_b"], p["ln1_g"], p["ln1_b"], p["router_w"],
                          p["router_b"], _pick_tile(n, 512))
    return _moe(h, comb, p["w_gate"], p["b_gate"], p["w_up"], p["b_up"], p["w_down"], p["b_down"],
                p["ln2_g"], p["ln2_b"], tb, tt, _slot_rows(tt))


def kernel(x_prompt, x_sample, cache_k_diff, cache_v_diff, cache_k_sparse, cache_v_sparse, cache_k_index, page_table,
           w_in, lambda_q1, lambda_k1, lambda_q2, lambda_k2, subln_g, w_out, ln1_g, ln1_b, router_w, router_b,
           w_gate, b_gate, w_up, b_up, w_down, b_down, ln2_g, ln2_b):
    assert w_in.shape[0] == DEPTH
    l = 0
    b, t, d = x_prompt.shape
    db = x_sample.shape[0]
    lam_init = 0.8 - 0.6 * math.exp(-0.3 * l)
    lam = (jnp.exp(jnp.sum(lambda_q1[l] * lambda_k1[l])) - jnp.exp(jnp.sum(lambda_q2[l] * lambda_k2[l]))
           + lam_init).reshape(1, 1)
    g = subln_g[l].reshape(1, -1)
    w_pad = _pad_w_in(w_in[l])
    params = _prep_params(dict(w_out=w_out, ln1_g=ln1_g, ln1_b=ln1_b, router_w=router_w, router_b=router_b,
                               w_gate=w_gate, b_gate=b_gate, w_up=w_up, b_up=b_up, w_down=w_down, b_down=b_down,
                               ln2_g=ln2_g, ln2_b=ln2_b), l)

    mix_a, mix_b, rows_p = _prompt_mixers(x_prompt, w_pad, lam, g, lam_init)
    n = b * t
    y_p = _post_layer(x_prompt.reshape(n, d), mix_a, mix_b, params, _pick_tile(n, 2048), _pick_tile(n, 512))

    caches = (cache_k_diff[l], cache_v_diff[l], cache_k_sparse[l], cache_v_sparse[l], cache_k_index[l])
    mix_a, mix_b, rows_s = _decode_mixers(x_sample, w_pad, lam, g, lam_init, caches, page_table,
                                          _pick_tile(page_table.shape[1], 16))
    y_s = _post_layer(x_sample.reshape(db, d), mix_a, mix_b, params, db, db)
    return (y_p.reshape(b, t, d), y_s.reshape(db, 1, d)) + rows_p + rows_s
```

```python
import functools
import math

import jax
import jax.numpy as jnp
from jax import lax
from jax.experimental import pallas as pl
from jax.experimental.pallas import tpu as pltpu

F32 = jnp.float32
BF16 = jnp.bfloat16
I32 = jnp.int32

A_HEADS = 4
A_HEAD_DIM = 64
B_HEADS = 8
B_KV_HEADS = 2
B_HEAD_DIM = 64
IDX_HEADS = 8
IDX_DIM = 64
TOPK_MAX = 256
N_EXPERTS = 32
TOP_K = 4
SWIGLU_LIMIT = 7.0
SWIGLU_ALPHA = 1.702
ROPE_THETA = 10000.0
LN_EPS = 1e-5
RMS_EPS = 1e-5
DEPTH = 1
DEEPNORM_ALPHA = (2.0 * DEPTH) ** 0.25
IDX_SCALE = (IDX_HEADS * IDX_DIM) ** -0.5

LANES = 128
HALF = 64
NEG_BIG = -1e30
INT_MIN = -(2 ** 31)
VMEM_LIMIT = 56 * 1024 * 1024

_C_QA, _C_KA, _C_VA, _C_QB, _C_KB, _C_VB, _C_QI, _C_KIWI = 0, 512, 1024, 1536, 2048, 2176, 2304, 2816
_IN_COLS = 2888
_IN_COLS_PAD = 2944


def _cparams(sem):
    return pltpu.CompilerParams(dimension_semantics=sem, vmem_limit_bytes=VMEM_LIMIT)


def _nt_dot(a, b):
    return lax.dot_general(a, b, (((1,), (1,)), ((), ())), preferred_element_type=F32)


def _tn_dot(a, b):
    return lax.dot_general(a, b, (((0,), (0,)), ((), ())), preferred_element_type=F32)


def _lane_iota(shape):
    return lax.broadcasted_iota(I32, shape, 1)


def _proj_kernel(x_ref, w_ref, cos_ref, sin_ref,
                 qa_ref, kaf_ref, kab_ref, vaf_ref, vab_ref, qb_ref, kbf_ref, kbb_ref, vbf_ref, vbb_ref,
                 qi_ref, kif_ref, kib_ref, wi_ref):
    xb = x_ref[...].astype(BF16)
    cos = cos_ref[...]
    sin = sin_ref[...]
    lane = _lane_iota(cos.shape)
    first = (lane % HALF) < (HALF // 2)

    def seg(c0, n):
        return jnp.dot(xb, w_ref[:, c0:c0 + n], preferred_element_type=F32)

    def rope(h):
        rot = jnp.where(first, pltpu.roll(h, LANES - HALF // 2, 1), pltpu.roll(h, HALF // 2, 1))
        return h * cos + rot * sin

    def blocks(h):
        return [h[:, k * LANES:(k + 1) * LANES] for k in range(h.shape[1] // LANES)]

    scale = A_HEAD_DIM ** -0.5 * math.log2(math.e)
    for k, h in enumerate(blocks(seg(_C_QA, 512))):
        qa_ref[:, k * LANES:(k + 1) * LANES] = (rope(h) * scale).astype(BF16)
    for k, h in enumerate(blocks(seg(_C_KA, 512))):
        r = rope(h)
        kaf_ref[:, k, :] = r
        kab_ref[:, k * LANES:(k + 1) * LANES] = r.astype(BF16)
    va = seg(_C_VA, 512)
    for k, h in enumerate(blocks(va)):
        vaf_ref[:, k, :] = h
    vab_ref[...] = va.astype(BF16)
    for k, h in enumerate(blocks(seg(_C_QB, 512))):
        qb_ref[:, k * LANES:(k + 1) * LANES] = (rope(h) * scale).astype(BF16)
    kbvb = seg(_C_KB, 256)
    kb = rope(kbvb[:, :LANES])
    kbb_ref[...] = kb.astype(BF16)
    vb = kbvb[:, LANES:]
    vbb_ref[...] = vb.astype(BF16)
    for ref, val in ((kbf_ref, kb.T), (vbf_ref, vb.T)):
        for g in range(B_KV_HEADS):
            ref[0, g] = val[g * B_HEAD_DIM:(g + 1) * B_HEAD_DIM]
    for k, h in enumerate(blocks(seg(_C_QI, 512))):
        qi_ref[:, k * LANES:(k + 1) * LANES] = rope(h).astype(BF16)
    kiwi = seg(_C_KIWI, LANES)
    ki = rope(kiwi)
    kif_ref[0] = ki.T[:IDX_DIM]
    kib_ref[...] = jnp.where(lane < HALF, ki, pltpu.roll(ki, HALF, 1)).astype(BF16)
    wi_ref[...] = pltpu.roll(kiwi, HALF, 1)[:, :IDX_HEADS]


def _project(x, w_pad, cos, sin, tm):
    n, d = x.shape
    t_tab = cos.shape[0]
    n_tab = t_tab // tm
    nseq = n // t_tab
    row = lambda i: (i, 0)
    row2 = lambda w, dt: (((n, w), dt), pl.BlockSpec((tm, w), row))
    diff_rows = (((n, A_HEADS, LANES), F32), pl.BlockSpec((tm, A_HEADS, LANES), lambda i: (i, 0, 0)))
    sparse_rows = (((nseq, B_KV_HEADS, B_HEAD_DIM, t_tab), F32),
                   pl.BlockSpec((1, B_KV_HEADS, B_HEAD_DIM, tm), lambda i: (i // n_tab, 0, 0, i % n_tab)))
    index_rows = (((nseq, IDX_DIM, t_tab), F32), pl.BlockSpec((1, IDX_DIM, tm), lambda i: (i // n_tab, 0, i % n_tab)))
    outs = [row2(512, BF16), diff_rows, row2(512, BF16), diff_rows, row2(512, BF16),
            row2(512, BF16), sparse_rows, row2(LANES, BF16), sparse_rows, row2(LANES, BF16),
            row2(512, BF16), index_rows, row2(LANES, BF16), row2(IDX_HEADS, F32)]
    return pl.pallas_call(
        _proj_kernel,
        grid=(n // tm,),
        in_specs=[pl.BlockSpec((tm, d), row),
                  pl.BlockSpec((d, _IN_COLS_PAD), lambda i: (0, 0)),
                  pl.BlockSpec((tm, LANES), lambda i: (i % n_tab, 0)),
                  pl.BlockSpec((tm, LANES), lambda i: (i % n_tab, 0))],
        out_specs=[spec for _, spec in outs],
        out_shape=[jax.ShapeDtypeStruct(s, dt) for (s, dt), _ in outs],
        compiler_params=_cparams(("parallel",)),
        name="proj",
    )(x, w_pad, cos, sin)


def _rope_tables(pos):
    inv = ROPE_THETA ** (-jnp.arange(0, HALF, 2, dtype=F32) / HALF)
    ang = pos.astype(F32)[:, None] * inv[None, :]
    cos = jnp.concatenate([jnp.cos(ang)] * 4, axis=-1)
    sin = jnp.concatenate([-jnp.sin(ang), jnp.sin(ang)] * 2, axis=-1)
    return cos, sin


def _diff_finish(acc, l, lam, g, lam_init, rows):
    o = acc[:rows] / l[:rows] - lam * (acc[rows:] / l[rows:])
    o = o * lax.rsqrt(jnp.mean(o * o, axis=-1, keepdims=True) + RMS_EPS)
    return o * g * (1.0 - lam_init)


ONES_ROWS = 16


def _with_ones_rows(vt):
    return jnp.concatenate([vt, jnp.ones(vt.shape[:-2] + (ONES_ROWS, vt.shape[-1]), vt.dtype)], axis=-2)


def _diff_kernel(lam_ref, q_ref, k_ref, vt_ref, g_ref, o_ref, m_scr, acc_scr, *, tq, lam_init):
    qi = pl.program_id(2)
    q = q_ref[0]
    lane = _lane_iota(q.shape)
    zero = jnp.zeros_like(q)
    qq = jnp.concatenate([jnp.where(lane < HALF, q, zero), jnp.where(lane >= HALF, q, zero)], axis=0)
    m_scr[...] = jnp.full(m_scr.shape, NEG_BIG, F32)
    acc_scr[...] = jnp.zeros(acc_scr.shape, F32)

    def step(j, diagonal):
        off = pl.multiple_of(j * tq, tq)
        s = _nt_dot(k_ref[0, pl.ds(off, tq), :], qq)
        if diagonal:
            kpos = lax.broadcasted_iota(I32, s.shape, 0)
            qpos = lax.broadcasted_iota(I32, s.shape, 1) % tq
            s = jnp.where(kpos <= qpos, s, NEG_BIG)
        m_old = m_scr[...]
        m_new = jnp.maximum(m_old, jnp.max(s, axis=0, keepdims=True))
        a = jnp.exp2(m_old - m_new)
        p = jnp.exp2(s - m_new)
        acc_scr[...] = a * acc_scr[...] + jnp.dot(vt_ref[0, 0, j], p.astype(BF16), preferred_element_type=F32)
        m_scr[...] = m_new

    def body(j, carry):
        step(j, False)
        return carry

    lax.fori_loop(0, qi, body, 0)
    step(qi, True)
    o = acc_scr[:LANES, :] * (1.0 / acc_scr[LANES:LANES + 1, :])
    o = o[:, :tq] - lam_ref[0, 0] * o[:, tq:]
    o = o * lax.rsqrt(jnp.mean(o * o, axis=0, keepdims=True) + RMS_EPS)
    o = o * g_ref[...] * (1.0 - lam_init)
    o_ref[0] = o.T.astype(o_ref.dtype)


def _diff_attention(lam, qa, ka, va, g, lam_init, tq):
    b, t, _ = qa.shape
    nblk = t // tq
    vt = _with_ones_rows(jnp.transpose(va.reshape(b, nblk, tq, A_HEADS, LANES), (0, 3, 1, 4, 2)))
    rows = LANES + ONES_ROWS
    return pl.pallas_call(
        functools.partial(_diff_kernel, tq=tq, lam_init=lam_init),
        grid=(b, A_HEADS, nblk),
        in_specs=[pl.BlockSpec(memory_space=pltpu.SMEM),
                  pl.BlockSpec((1, tq, LANES), lambda b_, h, i: (b_, i, h)),
                  pl.BlockSpec((1, t, LANES), lambda b_, h, i: (b_, 0, h)),
                  pl.BlockSpec((1, 1, nblk, rows, tq), lambda b_, h, i: (b_, h, 0, 0, 0)),
                  pl.BlockSpec((LANES, 1), lambda b_, h, i: (0, 0))],
        out_specs=pl.BlockSpec((1, tq, LANES), lambda b_, h, i: (b_, i, h)),
        out_shape=jax.ShapeDtypeStruct((b, t, A_HEADS * LANES), BF16),
        scratch_shapes=[pltpu.VMEM((1, 2 * tq), F32), pltpu.VMEM((rows, 2 * tq), F32)],
        compiler_params=_cparams(("parallel", "parallel", "arbitrary")),
        name="diff_attn",
    )(lam, qa, ka, vt, g.reshape(LANES, 1))


def _float_key(x):
    b = lax.bitcast_convert_type(x + 0.0, I32)
    return b ^ ((b >> 31) & 0x7FFFFFFF)


I16 = jnp.int16
I16_MIN = -(2 ** 15)


def _split_key(key):
    return (key >> 16).astype(I16), ((key & 0xFFFF) - 2 ** 15).astype(I16)


def _select_topk(key_scr, hi_scr, lo_scr, nch, ck, topk):
    nq = key_scr.shape[1]
    kf = float(topk)
    grp = 64
    assert key_scr.shape[0] // grp < 2 ** 15

    def counts(plane, thr, strict):
        thr_b = jnp.broadcast_to(thr.astype(plane.dtype), (grp, nq))
        one, zero = jnp.ones((), plane.dtype), jnp.zeros((), plane.dtype)

        def body(c, cnt):
            off = pl.multiple_of(c * ck, ck)
            for g in range(ck // grp):
                kk = plane[pl.ds(off + g * grp, grp), :]
                hit = (kk > thr_b) if strict else (kk >= thr_b)
                cnt = cnt + jnp.where(hit, one, zero)
            return cnt
        cnt = lax.fori_loop(0, nch, body, jnp.zeros((grp, nq), plane.dtype))
        return jnp.sum(cnt.astype(F32), axis=0, keepdims=True)

    def search16(plane, base):
        tau = jnp.where(base + counts(plane, jnp.zeros((1, nq), I32), False) >= kf, 0, I16_MIN).astype(I32)

        def bit_body(i, tau):
            cand = tau | jnp.left_shift(jnp.int32(1), 14 - i)
            return jnp.where(base + counts(plane, cand, False) >= kf, cand, tau)

        return lax.fori_loop(0, 15, bit_body, tau)

    tau_hi = search16(hi_scr, jnp.zeros((1, nq), F32))
    above = counts(hi_scr, tau_hi, True)
    hi_b = jnp.broadcast_to(tau_hi.astype(I16), (grp, nq))

    def mask_body(c, carry):
        off = pl.multiple_of(c * ck, ck)
        for g in range(ck // grp):
            sl = pl.ds(off + g * grp, grp)
            lo_scr[sl, :] = jnp.where(hi_scr[sl, :] == hi_b, lo_scr[sl, :], jnp.full((), I16_MIN, I16))
        return carry

    lax.fori_loop(0, nch, mask_body, 0)
    tau_lo = search16(lo_scr, above)
    tau = tau_hi * 2 ** 16 + (tau_lo + 2 ** 15)

    def bcast(v, rows=grp):
        return jnp.broadcast_to(v, (rows, nq))

    cnt_ge = counts(key_scr, tau, False)
    cnt_gt = counts(key_scr, tau, True)
    overfull = jnp.where((cnt_ge > kf) & (tau > INT_MIN), 1.0, 0.0)

    @pl.when(jnp.max(overfull) > 0.0)
    def _():
        tau_b = bcast(tau, LANES)
        need_b = bcast(kf - cnt_gt, LANES)
        lower = jnp.where(lax.broadcasted_iota(I32, (LANES, LANES), 0)
                          >= lax.broadcasted_iota(I32, (LANES, LANES), 1), 1.0, 0.0).astype(BF16)

        def body(c, run):
            off = pl.multiple_of(c * ck, ck)
            for g in range(ck // LANES):
                sl = pl.ds(off + g * LANES, LANES)
                kk = key_scr[sl, :]
                eq = kk == tau_b
                eqf = jnp.where(eq, 1.0, 0.0)
                pre = jnp.dot(lower, eqf.astype(BF16), preferred_element_type=F32)
                drop = eq & ((pre + run) > need_b)
                key_scr[sl, :] = jnp.where(drop, INT_MIN, kk)
                run = run + jnp.sum(eqf, axis=0, keepdims=True)
            return run

        lax.fori_loop(0, nch, body, jnp.zeros((1, nq), F32))

    return tau


def _stack_heads(q, n_heads):
    lane = _lane_iota((q.shape[0], LANES))
    zero = jnp.zeros((q.shape[0], LANES), q.dtype)
    parts = []
    for h in range(n_heads):
        blk = q[:, (h // 2) * LANES:(h // 2 + 1) * LANES]
        parts.append(jnp.where((lane < HALF) == (h % 2 == 0), blk, zero))
    return jnp.concatenate(parts, axis=0)


M_INIT = -1e29


def _sparse_kernel(qi_ref, wt_ref, ki_ref, qb_ref, kb_ref, vt_ref, o_ref, key_scr, hi_scr, lo_scr, m_scr, acc_scr,
                   *, tq, ck, topk):
    i = pl.program_id(1)
    nch = ((i + 1) * tq + ck - 1) // ck
    kpos0 = lax.broadcasted_iota(I32, (ck, tq), 0)
    qpos = i * tq + lax.broadcasted_iota(I32, (ck, tq), 1)

    qs = _stack_heads(qi_ref[0], IDX_HEADS)
    wt = wt_ref[0]

    def score_body(c, carry):
        off = pl.multiple_of(c * ck, ck)
        d = _nt_dot(ki_ref[0, pl.ds(off, ck), :], qs)
        sc = jnp.zeros((ck, tq), F32)
        for h in range(IDX_HEADS):
            sc = sc + jnp.maximum(d[:, h * tq:(h + 1) * tq], 0.0) * wt[h:h + 1, :]
        sc = sc * IDX_SCALE
        key = jnp.where(kpos0 + off <= qpos, _float_key(sc), INT_MIN)
        key_scr[pl.ds(off, ck), :] = key
        hi_scr[pl.ds(off, ck), :], lo_scr[pl.ds(off, ck), :] = _split_key(key)
        return carry

    lax.fori_loop(0, nch, score_body, 0)
    tau = _select_topk(key_scr, hi_scr, lo_scr, nch, ck, topk)
    tau_b = jnp.broadcast_to(tau, (ck, tq))

    qh = []
    qb = qb_ref[0]
    lane = _lane_iota((tq, LANES))
    zero = jnp.zeros((tq, LANES), qb.dtype)
    grp = B_HEADS // B_KV_HEADS
    for h in range(B_HEADS):
        blk = qb[:, (h // 2) * LANES:(h // 2 + 1) * LANES]
        if h % 2 != h // grp:
            blk = pltpu.roll(blk, HALF, 1)
        qh.append(jnp.where((lane >= HALF) == (h // grp == 1), blk, zero))
    qs2 = jnp.concatenate(qh, axis=0)
    m_scr[...] = jnp.full(m_scr.shape, M_INIT, F32)
    acc_scr[...] = jnp.zeros(acc_scr.shape, F32)

    def attn_body(c, carry):
        off = pl.multiple_of(c * ck, ck)
        sel = (key_scr[pl.ds(off, ck), :] >= tau_b) & (kpos0 + off <= qpos)
        bias = jnp.where(sel, 0.0, NEG_BIG)
        s = _nt_dot(kb_ref[0, pl.ds(off, ck), :], qs2)
        ps, scales = [], []
        for h in range(B_HEADS):
            hs = slice(h * tq, (h + 1) * tq)
            sh = s[:, hs] + bias
            m_old = m_scr[:, hs]
            m_new = jnp.maximum(m_old, jnp.max(sh, axis=0, keepdims=True))
            a = jnp.exp2(m_old - m_new)
            p = jnp.exp2(sh - m_new)
            m_scr[:, hs] = m_new
            ps.append(p.astype(BF16))
            scales.append(a)
        pv = jnp.dot(vt_ref[0, c], jnp.concatenate(ps, axis=1), preferred_element_type=F32)
        acc_scr[...] = jnp.concatenate(scales, axis=1) * acc_scr[...] + pv
        return carry

    lax.fori_loop(0, nch, attn_body, 0)
    o = acc_scr[:LANES, :] * (1.0 / acc_scr[LANES:LANES + 1, :])
    parts = [o[(h // grp) * HALF:(h // grp + 1) * HALF, h * tq:(h + 1) * tq] for h in range(B_HEADS)]
    o_ref[0] = jnp.concatenate(parts, axis=0).T.astype(o_ref.dtype)


def _sparse_attention(qi, wi, ki2, qb, kb, vb, tq, ck):
    b, t, _ = qi.shape
    topk = min(TOPK_MAX, t // 4)
    wt = jnp.transpose(wi, (0, 2, 1))
    vt = _with_ones_rows(jnp.transpose(vb.reshape(b, t // ck, ck, LANES), (0, 1, 3, 2)))
    rows = LANES + ONES_ROWS
    blk_q = lambda w: pl.BlockSpec((1, tq, w), lambda b_, i: (b_, i, 0))
    blk_k = pl.BlockSpec((1, t, LANES), lambda b_, i: (b_, 0, 0))
    return pl.pallas_call(
        functools.partial(_sparse_kernel, tq=tq, ck=ck, topk=topk),
        grid=(b, t // tq),
        in_specs=[blk_q(512), pl.BlockSpec((1, IDX_HEADS, tq), lambda b_, i: (b_, 0, i)), blk_k, blk_q(512), blk_k,
                  pl.BlockSpec((1, t // ck, rows, ck), lambda b_, i: (b_, 0, 0, 0))],
        out_specs=blk_q(512),
        out_shape=jax.ShapeDtypeStruct((b, t, 512), BF16),
        scratch_shapes=[pltpu.VMEM((t, tq), I32), pltpu.VMEM((t, tq), I16), pltpu.VMEM((t, tq), I16),
                        pltpu.VMEM((1, B_HEADS * tq), F32),
                        pltpu.VMEM((rows, B_HEADS * tq), F32)],
        compiler_params=_cparams(("parallel", "arbitrary")),
        name="sparse_attn",
    )(qi, wt, ki2, qb, kb, vt)


def _page_specs(n, block, pg):
    zeros = (0,) * (len(block) - 1)
    return [pl.BlockSpec(block, functools.partial(lambda i, b, j, pt: (pt[b, j * pg + i],) + zeros, i))
            for i in range(n)]


def _dec_diff_kernel(pt_ref, lam_ref, q_ref, kn_ref, vn_ref, g_ref, qi_ref, w_ref, kin_ref, *rest, pg, lam_init):
    k_refs, v_refs, ki_refs = rest[:pg], rest[pg:2 * pg], rest[2 * pg:3 * pg]
    o_ref, sc_ref, new_ref, m_scr, l_scr, acc_scr, kbuf, vbuf, kibuf = rest[3 * pg:]
    j = pl.program_id(1)

    @pl.when(j == 0)
    def _():
        m_scr[...] = jnp.full(m_scr.shape, NEG_BIG, F32)
        l_scr[...] = jnp.zeros(l_scr.shape, F32)
        acc_scr[...] = jnp.zeros(acc_scr.shape, F32)

    rows_pg = k_refs[0].shape[1]
    page = ki_refs[0].shape[2]
    for i in range(pg):
        kbuf[i * rows_pg:(i + 1) * rows_pg, :] = k_refs[i][0].astype(BF16)
        vbuf[i * rows_pg:(i + 1) * rows_pg, :] = v_refs[i][0].astype(BF16)
        kibuf[:, i * page:(i + 1) * page] = ki_refs[i][0].astype(BF16)

    q = q_ref[0]
    s = _nt_dot(q, kbuf[...])
    same_head = (_lane_iota(s.shape) % A_HEADS) == (lax.broadcasted_iota(I32, s.shape, 0) % A_HEADS)
    s = jnp.where(same_head, s, NEG_BIG)
    m_old = m_scr[...]
    m_new = jnp.maximum(m_old, jnp.max(s, axis=1, keepdims=True))
    a = jnp.exp2(m_old - m_new)
    p = jnp.exp2(s - m_new).astype(BF16)
    l_scr[...] = a * l_scr[...] + jnp.sum(p.astype(F32), axis=1, keepdims=True)
    acc_scr[...] = a * acc_scr[...] + jnp.dot(p, vbuf[...], preferred_element_type=F32)
    m_scr[...] = m_new

    qi = qi_ref[0]
    w = w_ref[0]
    d = jnp.dot(qi, kibuf[...], preferred_element_type=F32)
    sc_ref[0] = jnp.sum(jnp.maximum(d, 0.0) * w, axis=0, keepdims=True) * IDX_SCALE

    @pl.when(j == pl.num_programs(1) - 1)
    def _():
        s_new = jnp.sum(q.astype(F32) * kn_ref[0], axis=1, keepdims=True)
        m_old = m_scr[...]
        m_new = jnp.maximum(m_old, s_new)
        a = jnp.exp2(m_old - m_new)
        pn = jnp.exp2(s_new - m_new)
        l = a * l_scr[...] + pn
        acc = a * acc_scr[...] + pn * vn_ref[0]
        o = _diff_finish(acc, l, lam_ref[0, 0], g_ref[...], lam_init, A_HEADS)
        o_ref[0] = jnp.concatenate([o, jnp.zeros_like(o)], axis=0)
        dn = jnp.sum(qi.astype(F32) * kin_ref[0], axis=1, keepdims=True)
        scn = jnp.sum(jnp.maximum(dn, 0.0) * w, axis=0, keepdims=True) * IDX_SCALE
        new_ref[0] = jnp.broadcast_to(scn, new_ref.shape[1:])


def _dec_diff_attention(pt, lam, qm, knew, vnew, g, qi, wi, kinew, kpages, vpages, kipages, lam_init, pg):
    db, npg = pt.shape
    rows_pg = kpages.shape[1]
    page = kipages.shape[2]
    per_b = lambda shape: pl.BlockSpec(shape, lambda b, j, pt_: (b,) + (0,) * (len(shape) - 1))
    return pl.pallas_call(
        functools.partial(_dec_diff_kernel, pg=pg, lam_init=lam_init),
        grid_spec=pltpu.PrefetchScalarGridSpec(
            num_scalar_prefetch=1,
            grid=(db, npg // pg),
            in_specs=[pl.BlockSpec(memory_space=pltpu.SMEM), per_b((1, 8, LANES)), per_b((1, 8, LANES)),
                      per_b((1, 8, LANES)), pl.BlockSpec((1, LANES), lambda b, j, pt_: (0, 0)),
                      per_b((1, IDX_HEADS, IDX_DIM)), per_b((1, IDX_HEADS, 1)), per_b((1, 1, IDX_DIM))]
                     + _page_specs(pg, (1,) + kpages.shape[1:], pg) + _page_specs(pg, (1,) + vpages.shape[1:], pg)
                     + _page_specs(pg, (1,) + kipages.shape[1:], pg),
            out_specs=[per_b((1, 8, LANES)), pl.BlockSpec((1, 1, pg * page), lambda b, j, pt_: (b, 0, j)),
                       per_b((1, 1, LANES))],
            scratch_shapes=[pltpu.VMEM((8, 1), F32), pltpu.VMEM((8, 1), F32), pltpu.VMEM((8, LANES), F32),
                            pltpu.VMEM((pg * rows_pg, LANES), BF16), pltpu.VMEM((pg * rows_pg, LANES), BF16),
                            pltpu.VMEM((IDX_DIM, pg * page), BF16)],
        ),
        out_shape=[jax.ShapeDtypeStruct((db, 8, LANES), F32), jax.ShapeDtypeStruct((db, 1, npg * page), F32),
                   jax.ShapeDtypeStruct((db, 1, LANES), F32)],
        compiler_params=_cparams(("parallel", "arbitrary")),
        name="dec_diff_attn",
    )(pt, lam, qm, knew, vnew, g, qi, wi, kinew, *([kpages] * pg), *([vpages] * pg), *([kipages] * pg))


def _dec_select_kernel(sc_ref, key_ref, tau_ref, hi_scr, lo_scr, *, n_keys, topk):
    width, nq = sc_ref.shape
    kpos = lax.broadcasted_iota(I32, (width, nq), 0)
    key = jnp.where(kpos < n_keys, _float_key(sc_ref[...]), INT_MIN)
    key_ref[...] = key
    hi_scr[...], lo_scr[...] = _split_key(key)
    tau = _select_topk(key_ref, hi_scr, lo_scr, width // LANES, LANES, topk)
    tau_ref[...] = jnp.broadcast_to(tau, tau_ref.shape)


def _dec_select(scores_t, n_keys, topk):
    width, nq = scores_t.shape
    return pl.pallas_call(
        functools.partial(_dec_select_kernel, n_keys=n_keys, topk=topk),
        out_shape=[jax.ShapeDtypeStruct((width, nq), I32), jax.ShapeDtypeStruct((8, nq), I32)],
        scratch_shapes=[pltpu.VMEM((width, nq), I16), pltpu.VMEM((width, nq), I16)],
        compiler_params=pltpu.CompilerParams(vmem_limit_bytes=VMEM_LIMIT),
        name="dec_select",
    )(scores_t)


def _dec_sparse_kernel(pt_ref, q_ref, key_ref, keyn_ref, tau_ref, kn_ref, vn_ref, *rest, pg):
    k_refs, v_refs = rest[:pg], rest[pg:2 * pg]
    o_ref, m_scr, l_scr, acc_scr, kbuf, vbuf = rest[2 * pg:]
    j = pl.program_id(1)

    @pl.when(j == 0)
    def _():
        m_scr[...] = jnp.full(m_scr.shape, NEG_BIG, F32)
        l_scr[...] = jnp.zeros(l_scr.shape, F32)
        acc_scr[...] = jnp.zeros(acc_scr.shape, F32)

    page = k_refs[0].shape[3]
    for i in range(pg):
        kbuf[:, :, i * page:(i + 1) * page] = k_refs[i][0].astype(BF16)
        vbuf[:, :, i * page:(i + 1) * page] = v_refs[i][0].astype(BF16)
    tau = tau_ref[0][:, 0:1]
    sel = jnp.broadcast_to(key_ref[0] >= tau, (8, pg * page))
    for g in range(B_KV_HEADS):
        q = q_ref[0, g]
        s = jnp.dot(q, kbuf[g], preferred_element_type=F32)
        s = jnp.where(sel, s, NEG_BIG)
        m_old = m_scr[g]
        m_new = jnp.maximum(m_old, jnp.max(s, axis=1, keepdims=True))
        a = jnp.exp2(m_old - m_new)
        p = jnp.where(sel, jnp.exp2(s - m_new), 0.0).astype(BF16)
        l_scr[g] = a * l_scr[g] + jnp.sum(p.astype(F32), axis=1, keepdims=True)
        acc_scr[g] = a * acc_scr[g] + _nt_dot(p, vbuf[g])
        m_scr[g] = m_new

    @pl.when(j == pl.num_programs(1) - 1)
    def _():
        sel_new = keyn_ref[0][:, 0:1] >= tau
        for g in range(B_KV_HEADS):
            s_new = jnp.sum(q_ref[0, g].astype(F32) * kn_ref[0, g], axis=1, keepdims=True)
            s_new = jnp.where(sel_new, s_new, NEG_BIG)
            m_old = m_scr[g]
            m_new = jnp.maximum(m_old, s_new)
            a = jnp.exp2(m_old - m_new)
            pn = jnp.where(sel_new, jnp.exp2(s_new - m_new), 0.0)
            l = a * l_scr[g] + pn
            o_ref[0, g] = (a * acc_scr[g] + pn * vn_ref[0, g]) / l


def _dec_sparse_attention(pt, q, keys, tau, kbnew, vbnew, kpages, vpages, pg):
    db, npg = pt.shape
    page = kpages.shape[3]
    per_b = lambda shape: pl.BlockSpec(shape, lambda b, j, pt_: (b,) + (0,) * (len(shape) - 1))
    blk = (1,) + kpages.shape[1:]
    return pl.pallas_call(
        functools.partial(_dec_sparse_kernel, pg=pg),
        grid_spec=pltpu.PrefetchScalarGridSpec(
            num_scalar_prefetch=1,
            grid=(db, npg // pg),
            in_specs=[per_b((1, B_KV_HEADS, 8, B_HEAD_DIM)),
                      pl.BlockSpec((1, 1, pg * page), lambda b, j, pt_: (b, 0, j)),
                      pl.BlockSpec((1, 1, LANES), lambda b, j, pt_: (b, 0, npg * page // LANES)),
                      per_b((1, 1, LANES)), per_b((1, B_KV_HEADS, 1, B_HEAD_DIM)),
                      per_b((1, B_KV_HEADS, 1, B_HEAD_DIM))]
                     + _page_specs(pg, blk, pg) + _page_specs(pg, blk, pg),
            out_specs=per_b((1, B_KV_HEADS, 8, B_HEAD_DIM)),
            scratch_shapes=[pltpu.VMEM((B_KV_HEADS, 8, 1), F32), pltpu.VMEM((B_KV_HEADS, 8, 1), F32),
                            pltpu.VMEM((B_KV_HEADS, 8, B_HEAD_DIM), F32),
                            pltpu.VMEM((B_KV_HEADS, B_HEAD_DIM, pg * page), BF16),
                            pltpu.VMEM((B_KV_HEADS, B_HEAD_DIM, pg * page), BF16)],
        ),
        out_shape=jax.ShapeDtypeStruct((db, B_KV_HEADS, 8, B_HEAD_DIM), F32),
        compiler_params=_cparams(("parallel", "arbitrary")),
        name="dec_sparse_attn",
    )(pt, q, keys, keys, tau, kbnew, vbnew, *([kpages] * pg), *([vpages] * pg))


def _layer_norm(x, g, b):
    mu = jnp.mean(x, axis=-1, keepdims=True)
    xc = x - mu
    var = jnp.mean(xc * xc, axis=-1, keepdims=True)
    return xc * lax.rsqrt(var + LN_EPS) * g + b


def _post_kernel(x_ref, ma_ref, mb_ref, woa_ref, wob_ref, g_ref, b_ref, rw_ref, rb_ref, h_ref, c_ref):
    y = (DEEPNORM_ALPHA * x_ref[...]
         + jnp.dot(ma_ref[...], woa_ref[...], preferred_element_type=F32)
         + jnp.dot(mb_ref[...], wob_ref[...], preferred_element_type=F32))
    h = _layer_norm(y, g_ref[...], b_ref[...])
    h_ref[...] = h
    logits = jnp.dot(h, rw_ref[...], preferred_element_type=F32, precision=lax.Precision.HIGHEST) + rb_ref[...]
    lane = _lane_iota(logits.shape)
    work = logits
    sel = jnp.zeros(logits.shape, jnp.bool_)
    for _ in range(TOP_K):
        mx = jnp.max(work, axis=-1, keepdims=True)
        idx = jnp.min(jnp.where(work == mx, lane, N_EXPERTS), axis=-1, keepdims=True)
        pick = lane == idx
        sel = sel | pick
        work = jnp.where(pick, -jnp.inf, work)
    e = jnp.where(sel, jnp.exp(logits - jnp.max(logits, axis=-1, keepdims=True)), 0.0)
    c_ref[...] = e / jnp.sum(e, axis=-1, keepdims=True)


def _post_mixer(x, mix_a, mix_b, wo_a, wo_b, g, b, rw, rb, tm):
    n, d = x.shape
    row = lambda w: pl.BlockSpec((tm, w), lambda i: (i, 0))
    full = lambda a: pl.BlockSpec(a.shape, lambda i: (0, 0))
    return pl.pallas_call(
        _post_kernel,
        grid=(n // tm,),
        in_specs=[row(d), row(mix_a.shape[1]), row(mix_b.shape[1]), full(wo_a), full(wo_b), full(g), full(b),
                  full(rw), full(rb)],
        out_specs=[row(d), row(N_EXPERTS)],
        out_shape=[jax.ShapeDtypeStruct((n, d), F32), jax.ShapeDtypeStruct((n, N_EXPERTS), F32)],
        compiler_params=_cparams(("parallel",)),
        name="post_mixer",
    )(x, mix_a, mix_b, wo_a, wo_b, g, b, rw, rb)


def _moe_kernel(cnt_ref, h_ref, comb_ref, upper_ref, wg_ref, bg_ref, wu_ref, bu_ref, wd_ref, bd_ref,
                g_ref, b_ref, o_ref, xs_scr, pm_scr, gs_scr, ys_scr, *, tt, cs, nsub):
    i = pl.program_id(0)
    e = pl.program_id(1)
    n_exp = pl.num_programs(1)

    @pl.when(e == 0)
    def _():
        o_ref[...] = jnp.zeros(o_ref.shape, F32)

    slot0 = lax.broadcasted_iota(I32, (cs, tt), 0).astype(F32) + 1.0

    def onehot_of(j, first):
        gate = comb_ref[0, 0, pl.ds(j, 1), :]
        sel = gate > 0.0
        self_ = jnp.broadcast_to(jnp.where(sel, 1.0, 0.0), (8, tt)).astype(BF16)
        rank = jnp.dot(self_, upper_ref[...], preferred_element_type=F32)[0:1]
        onehot = jnp.broadcast_to(sel, (cs, tt)) & (jnp.broadcast_to(rank, (cs, tt)) == slot0 + first)
        return onehot, jnp.broadcast_to(gate, (cs, tt))

    def expert(xs):
        gg = jnp.dot(xs, wg_ref[0], preferred_element_type=F32) + bg_ref[0]
        uu = jnp.dot(xs, wu_ref[0], preferred_element_type=F32) + bu_ref[0]
        gg = jnp.minimum(gg, SWIGLU_LIMIT)
        uu = jnp.clip(uu, -SWIGLU_LIMIT, SWIGLU_LIMIT)
        hh = gg * jax.nn.sigmoid(SWIGLU_ALPHA * gg) * (uu + 1.0)
        return jnp.dot(hh.astype(BF16), wd_ref[0], preferred_element_type=F32) + bd_ref[0]

    odd = e % 2
    half = pl.ds(pl.multiple_of(odd * cs, 16), cs)
    for j in range(nsub):
        onehot, gate_b = onehot_of(j, 0.0)
        pm = jnp.where(onehot, 1.0, 0.0).astype(BF16)
        pm_scr[j, half, :] = pm
        xs_scr[j * cs:(j + 1) * cs, :] = jnp.dot(pm, h_ref[j * tt:(j + 1) * tt, :].astype(BF16),
                                                 preferred_element_type=F32).astype(BF16)
        gs_scr[j * cs:(j + 1) * cs, :] = jnp.sum(jnp.where(onehot, gate_b, 0.0), axis=1, keepdims=True)
    ys = (expert(xs_scr[...]) * gs_scr[...]).astype(BF16)
    for j in range(nsub):
        ys_scr[j, half, :] = ys[j * cs:(j + 1) * cs]

    @pl.when(odd == 1)
    def _():
        for j in range(nsub):
            o_ref[j * tt:(j + 1) * tt, :] += _tn_dot(pm_scr[j], ys_scr[j])

    def sub_body(j, carry):
        n = cnt_ref[(e * pl.num_programs(0) + i) * nsub + j]

        @pl.when(n > cs)
        def _():
            off = pl.multiple_of(j * tt, tt)
            xj = h_ref[pl.ds(off, tt), :].astype(BF16)

            def chunk_body(c, carry2):
                onehot, gate_b = onehot_of(j, (c * cs).astype(F32))
                pm = jnp.where(onehot, 1.0, 0.0).astype(BF16)
                y = expert(jnp.dot(pm, xj, preferred_element_type=F32).astype(BF16))
                gs = jnp.sum(jnp.where(onehot, gate_b, 0.0), axis=1, keepdims=True)
                o_ref[pl.ds(off, tt), :] += _tn_dot(pm, (y * gs).astype(BF16))
                return carry2

            lax.fori_loop(1, (n + cs - 1) // cs, chunk_body, 0)

        return carry

    lax.fori_loop(0, nsub, sub_body, 0)

    @pl.when(e == n_exp - 1)
    def _():
        o_ref[...] = _layer_norm(DEEPNORM_ALPHA * h_ref[...] + o_ref[...], g_ref[...], b_ref[...])


def _moe(h, comb, wg, bg, wu, bu, wd, bd, g, b, tb, tt, cs):
    n, d = h.shape
    n_exp, _, f = wg.shape
    assert n_exp % 2 == 0
    nsub = tb // tt
    nblk = n // tb
    comb_t = comb.T
    counts = jnp.sum((comb_t > 0.0).reshape(n_exp * nblk * nsub, tt), axis=-1).astype(I32)
    comb_t = comb_t.reshape(n_exp, nblk, nsub, tt)
    upper = (jnp.arange(tt)[:, None] <= jnp.arange(tt)[None, :]).astype(BF16)
    wspec = lambda a, b_: pl.BlockSpec((1, a, b_), lambda i, e, c: (e, 0, 0))
    vec = pl.BlockSpec((1, d), lambda i, e, c: (0, 0))
    return pl.pallas_call(
        functools.partial(_moe_kernel, tt=tt, cs=cs, nsub=nsub),
        grid_spec=pltpu.PrefetchScalarGridSpec(
            num_scalar_prefetch=1,
            grid=(nblk, n_exp),
            in_specs=[pl.BlockSpec((tb, d), lambda i, e, c: (i, 0), pipeline_mode=pl.Buffered(1)),
                      pl.BlockSpec((1, 1, nsub, tt), lambda i, e, c: (e, i, 0, 0)),
                      pl.BlockSpec((tt, tt), lambda i, e, c: (0, 0)),
                      wspec(d, f), wspec(1, f), wspec(d, f), wspec(1, f), wspec(f, d), wspec(1, d), vec, vec],
            out_specs=pl.BlockSpec((tb, d), lambda i, e, c: (i, 0)),
            scratch_shapes=[pltpu.VMEM((nsub * cs, d), BF16), pltpu.VMEM((nsub, 2 * cs, tt), BF16),
                            pltpu.VMEM((nsub * cs, 1), F32), pltpu.VMEM((nsub, 2 * cs, d), BF16)],
        ),
        out_shape=jax.ShapeDtypeStruct((n, d), F32),
        compiler_params=_cparams(("parallel", "arbitrary")),
        name="moe",
    )(counts, h, comb_t, upper, wg, bg.reshape(n_exp, 1, f), wu, bu.reshape(n_exp, 1, f),
      wd, bd.reshape(n_exp, 1, d), g, b)


def _pad_w_in(w_in):
    d = w_in.shape[0]
    return jnp.concatenate([w_in, jnp.zeros((d, _IN_COLS_PAD - _IN_COLS), w_in.dtype)], axis=1).astype(BF16)


def _prep_params(p, l=0):
    half = p["w_out"].shape[1] // 2
    vec = lambda a: a[l].reshape(1, -1)
    return {
        "wo_a": p["w_out"][l, :half].astype(BF16), "wo_b": p["w_out"][l, half:].astype(BF16),
        "ln1_g": vec(p["ln1_g"]), "ln1_b": vec(p["ln1_b"]), "ln2_g": vec(p["ln2_g"]), "ln2_b": vec(p["ln2_b"]),
        "router_w": p["router_w"][l], "router_b": vec(p["router_b"]),
        "w_gate": p["w_gate"][l].astype(BF16), "b_gate": p["b_gate"][l],
        "w_up": p["w_up"][l].astype(BF16), "b_up": p["b_up"][l],
        "w_down": p["w_down"][l].astype(BF16), "b_down": p["b_down"][l],
    }


def _pick_tile(n, pref):
    t = min(n, pref)
    while n % t:
        t //= 2
    return t


def _token_major(a):
    a = jnp.moveaxis(a, -1, 1)
    return a.reshape((a.shape[0] * a.shape[1],) + a.shape[2:])


def _cache_rows(kaf, vaf, kbf, vbf, kif, b, t):
    return (kaf.reshape(1, b, t, A_HEADS, 2 * A_HEAD_DIM), vaf.reshape(1, b, t, A_HEADS, 2 * A_HEAD_DIM),
            _token_major(kbf).reshape(1, b, t, B_KV_HEADS, B_HEAD_DIM),
            _token_major(vbf).reshape(1, b, t, B_KV_HEADS, B_HEAD_DIM),
            _token_major(kif).reshape(1, b, t, IDX_DIM))


def _prompt_mixers(x_prompt, w_pad, lam, subln_g, lam_init):
    b, t, d = x_prompt.shape
    n = b * t
    cos, sin = _rope_tables(jnp.arange(t))
    (qa, kaf, kab, vaf, vab, qb, kbf, kbb, vbf, vbb, qi, kif, kib, wi) = _project(
        x_prompt.reshape(n, d), w_pad, cos, sin, _pick_tile(t, 512))
    r3 = lambda a: a.reshape(b, t, a.shape[-1])
    mix_a = _diff_attention(lam, r3(qa), r3(kab), r3(vab), subln_g, lam_init, _pick_tile(t, 1024))
    mix_b = _sparse_attention(r3(qi), r3(wi), r3(kib), r3(qb), r3(kbb), r3(vbb), _pick_tile(t, 512),
                              _pick_tile(t, 512))
    return mix_a.reshape(n, -1), mix_b.reshape(n, -1), _cache_rows(kaf, vaf, kbf, vbf, kif, b, t)


def _decode_mixers(x_sample, w_pad, lam, subln_g, lam_init, caches, page_table, pg):
    ck_diff, cv_diff, ck_sparse, cv_sparse, ck_index = caches
    db, _, d = x_sample.shape
    n_pool, page = ck_diff.shape[:2]
    npg = page_table.shape[1]
    past = npg * page
    cos, sin = _rope_tables(jnp.full((db,), past))
    (qa, kaf, _, vaf, _, qb, kbf, _, vbf, _, qi, kif, _, wi) = _project(x_sample.reshape(db, d), w_pad, cos, sin, db)

    qa3 = qa.reshape(db, A_HEADS, LANES)
    lane = _lane_iota(qa3.shape[1:])[None]
    zero = jnp.zeros_like(qa3)
    qm = jnp.concatenate([jnp.where(lane < HALF, qa3, zero), jnp.where(lane >= HALF, qa3, zero)], axis=1)
    twice = lambda a: jnp.concatenate([a.reshape(db, A_HEADS, LANES)] * 2, axis=1)
    out_a, scores, sc_new = _dec_diff_attention(
        page_table, lam, qm, twice(kaf), twice(vaf), subln_g,
        qi.reshape(db, IDX_HEADS, IDX_DIM), wi.reshape(db, IDX_HEADS, 1), _token_major(kif).reshape(db, 1, IDX_DIM),
        ck_diff.reshape(n_pool, page * A_HEADS, LANES), cv_diff.reshape(n_pool, page * A_HEADS, LANES),
        jnp.transpose(ck_index, (0, 2, 1)), lam_init, pg)
    mix_a = out_a[:, :A_HEADS].reshape(db, A_HEADS * LANES).astype(BF16)
    sc_all = jnp.concatenate([scores[:, 0], sc_new[:, 0, :1], jnp.zeros((db, LANES - 1), F32)], axis=1)
    keys_t, tau = _dec_select(sc_all.T, past + 1, min(TOPK_MAX, (past + 1) // 4))
    keys = keys_t.T
    tau = jnp.broadcast_to(tau[0].reshape(db, 1, 1), (db, 1, LANES))

    grp = B_HEADS // B_KV_HEADS
    qb4 = qb.reshape(db, B_KV_HEADS, grp, B_HEAD_DIM)
    qs = jnp.concatenate([qb4, jnp.zeros((db, B_KV_HEADS, 8 - grp, B_HEAD_DIM), qb.dtype)], axis=2)
    out_b = _dec_sparse_attention(page_table, qs, keys.reshape(db, 1, -1), tau,
                                  _token_major(kbf).reshape(db, B_KV_HEADS, 1, B_HEAD_DIM),
                                  _token_major(vbf).reshape(db, B_KV_HEADS, 1, B_HEAD_DIM),
                                  jnp.transpose(ck_sparse, (0, 2, 3, 1)), jnp.transpose(cv_sparse, (0, 2, 3, 1)),
                                  _pick_tile(npg, 2 * pg))
    mix_b = out_b[:, :, :grp].reshape(db, B_HEADS * B_HEAD_DIM).astype(BF16)
    return mix_a, mix_b, _cache_rows(kaf, vaf, kbf, vbf, kif, db, 1)


def _slot_rows(tt):
    pack = 16
    want = tt * TOP_K // N_EXPERTS
    want += want // 4
    return max(pack, -(-want // pack) * pack)


def _post_layer(x, mix_a, mix_b, p, tb, tt):
    n, d = x.shape
    h, comb = _post_mixer(x, mix_a, mix_b, p["wo_a"], p["wo_b"], p["ln1_g"], p["ln1_b"], p["router_w"],
                          p["router_b"], _pick_tile(n, 512))
    return _moe(h, comb, p["w_gate"], p["b_gate"], p["w_up"], p["b_up"], p["w_down"], p["b_down"],
                p["ln2_g"], p["ln2_b"], tb, tt, _slot_rows(tt))


def kernel(x_prompt, x_sample, cache_k_diff, cache_v_diff, cache_k_sparse, cache_v_sparse, cache_k_index, page_table,
           w_in, lambda_q1, lambda_k1, lambda_q2, lambda_k2, subln_g, w_out, ln1_g, ln1_b, router_w, router_b,
           w_gate, b_gate, w_up, b_up, w_down, b_down, ln2_g, ln2_b):
    assert w_in.shape[0] == DEPTH
    l = 0
    b, t, d = x_prompt.shape
    db = x_sample.shape[0]
    lam_init = 0.8 - 0.6 * math.exp(-0.3 * l)
    lam = (jnp.exp(jnp.sum(lambda_q1[l] * lambda_k1[l])) - jnp.exp(jnp.sum(lambda_q2[l] * lambda_k2[l]))
           + lam_init).reshape(1, 1)
    g = subln_g[l].reshape(1, -1)
    w_pad = _pad_w_in(w_in[l])
    params = _prep_params(dict(w_out=w_out, ln1_g=ln1_g, ln1_b=ln1_b, router_w=router_w, router_b=router_b,
                               w_gate=w_gate, b_gate=b_gate, w_up=w_up, b_up=b_up, w_down=w_down, b_down=b_down,
                               ln2_g=ln2_g, ln2_b=ln2_b), l)

    mix_a, mix_b, rows_p = _prompt_mixers(x_prompt, w_pad, lam, g, lam_init)
    n = b * t
    y_p = _post_layer(x_prompt.reshape(n, d), mix_a, mix_b, params, _pick_tile(n, 2048), _pick_tile(n, 512))

    caches = (cache_k_diff[l], cache_v_diff[l], cache_k_sparse[l], cache_v_sparse[l], cache_k_index[l])
    mix_a, mix_b, rows_s = _decode_mixers(x_sample, w_pad, lam, g, lam_init, caches, page_table,
                                          _pick_tile(page_table.shape[1], 16))
    y_s = _post_layer(x_sample.reshape(db, d), mix_a, mix_b, params, db, db)
    return (y_p.reshape(b, t, d), y_s.reshape(db, 1, d)) + rows_p + rows_s
```

```python
import functools
import math

import jax
import jax.numpy as jnp
from jax import lax
from jax.experimental import pallas as pl
from jax.experimental.pallas import tpu as pltpu

F32 = jnp.float32
BF16 = jnp.bfloat16
I32 = jnp.int32

A_HEADS = 4
A_HEAD_DIM = 64
B_HEADS = 8
B_KV_HEADS = 2
B_HEAD_DIM = 64
IDX_HEADS = 8
IDX_DIM = 64
TOPK_MAX = 256
N_EXPERTS = 32
TOP_K = 4
SWIGLU_LIMIT = 7.0
SWIGLU_ALPHA = 1.702
ROPE_THETA = 10000.0
LN_EPS = 1e-5
RMS_EPS = 1e-5
DEPTH = 1
DEEPNORM_ALPHA = (2.0 * DEPTH) ** 0.25
IDX_SCALE = (IDX_HEADS * IDX_DIM) ** -0.5

LANES = 128
HALF = 64
NEG_BIG = -1e30
INT_MIN = -(2 ** 31)
VMEM_LIMIT = 56 * 1024 * 1024

_C_QA, _C_KA, _C_VA, _C_QB, _C_KB, _C_VB, _C_QI, _C_KIWI = 0, 512, 1024, 1536, 2048, 2176, 2304, 2816
_IN_COLS = 2888
_IN_COLS_PAD = 2944


def _cparams(sem):
    return pltpu.CompilerParams(dimension_semantics=sem, vmem_limit_bytes=VMEM_LIMIT)


def _nt_dot(a, b):
    return lax.dot_general(a, b, (((1,), (1,)), ((), ())), preferred_element_type=F32)


def _tn_dot(a, b):
    return lax.dot_general(a, b, (((0,), (0,)), ((), ())), preferred_element_type=F32)


def _lane_iota(shape):
    return lax.broadcasted_iota(I32, shape, 1)


def _proj_kernel(x_ref, w_ref, cos_ref, sin_ref,
                 qa_ref, kaf_ref, kab_ref, vaf_ref, vab_ref, qb_ref, kbf_ref, kbb_ref, vbf_ref, vbb_ref,
                 qi_ref, kif_ref, kib_ref, wi_ref):
    xb = x_ref[...].astype(BF16)
    cos = cos_ref[...]
    sin = sin_ref[...]
    lane = _lane_iota(cos.shape)
    first = (lane % HALF) < (HALF // 2)

    def seg(c0, n):
        return jnp.dot(xb, w_ref[:, c0:c0 + n], preferred_element_type=F32)

    def rope(h):
        rot = jnp.where(first, pltpu.roll(h, LANES - HALF // 2, 1), pltpu.roll(h, HALF // 2, 1))
        return h * cos + rot * sin

    def blocks(h):
        return [h[:, k * LANES:(k + 1) * LANES] for k in range(h.shape[1] // LANES)]

    scale = A_HEAD_DIM ** -0.5 * math.log2(math.e)
    for k, h in enumerate(blocks(seg(_C_QA, 512))):
        qa_ref[:, k * LANES:(k + 1) * LANES] = (rope(h) * scale).astype(BF16)
    for k, h in enumerate(blocks(seg(_C_KA, 512))):
        r = rope(h)
        kaf_ref[:, k, :] = r
        kab_ref[:, k * LANES:(k + 1) * LANES] = r.astype(BF16)
    va = seg(_C_VA, 512)
    for k, h in enumerate(blocks(va)):
        vaf_ref[:, k, :] = h
    vab_ref[...] = va.astype(BF16)
    for k, h in enumerate(blocks(seg(_C_QB, 512))):
        qb_ref[:, k * LANES:(k + 1) * LANES] = (rope(h) * scale).astype(BF16)
    kbvb = seg(_C_KB, 256)
    kb = rope(kbvb[:, :LANES])
    kbb_ref[...] = kb.astype(BF16)
    vb = kbvb[:, LANES:]
    vbb_ref[...] = vb.astype(BF16)
    for ref, val in ((kbf_ref, kb.T), (vbf_ref, vb.T)):
        for g in range(B_KV_HEADS):
            ref[0, g] = val[g * B_HEAD_DIM:(g + 1) * B_HEAD_DIM]
    for k, h in enumerate(blocks(seg(_C_QI, 512))):
        qi_ref[:, k * LANES:(k + 1) * LANES] = rope(h).astype(BF16)
    kiwi = seg(_C_KIWI, LANES)
    ki = rope(kiwi)
    kif_ref[0] = ki.T[:IDX_DIM]
    kib_ref[...] = jnp.where(lane < HALF, ki, pltpu.roll(ki, HALF, 1)).astype(BF16)
    wi_ref[...] = pltpu.roll(kiwi, HALF, 1)[:, :IDX_HEADS]


def _project(x, w_pad, cos, sin, tm):
    n, d = x.shape
    t_tab = cos.shape[0]
    n_tab = t_tab // tm
    nseq = n // t_tab
    row = lambda i: (i, 0)
    row2 = lambda w, dt: (((n, w), dt), pl.BlockSpec((tm, w), row))
    diff_rows = (((n, A_HEADS, LANES), F32), pl.BlockSpec((tm, A_HEADS, LANES), lambda i: (i, 0, 0)))
    sparse_rows = (((nseq, B_KV_HEADS, B_HEAD_DIM, t_tab), F32),
                   pl.BlockSpec((1, B_KV_HEADS, B_HEAD_DIM, tm), lambda i: (i // n_tab, 0, 0, i % n_tab)))
    index_rows = (((nseq, IDX_DIM, t_tab), F32), pl.BlockSpec((1, IDX_DIM, tm), lambda i: (i // n_tab, 0, i % n_tab)))
    outs = [row2(512, BF16), diff_rows, row2(512, BF16), diff_rows, row2(512, BF16),
            row2(512, BF16), sparse_rows, row2(LANES, BF16), sparse_rows, row2(LANES, BF16),
            row2(512, BF16), index_rows, row2(LANES, BF16), row2(IDX_HEADS, F32)]
    return pl.pallas_call(
        _proj_kernel,
        grid=(n // tm,),
        in_specs=[pl.BlockSpec((tm, d), row),
                  pl.BlockSpec((d, _IN_COLS_PAD), lambda i: (0, 0)),
                  pl.BlockSpec((tm, LANES), lambda i: (i % n_tab, 0)),
                  pl.BlockSpec((tm, LANES), lambda i: (i % n_tab, 0))],
        out_specs=[spec for _, spec in outs],
        out_shape=[jax.ShapeDtypeStruct(s, dt) for (s, dt), _ in outs],
        compiler_params=_cparams(("parallel",)),
        name="proj",
    )(x, w_pad, cos, sin)


def _rope_tables(pos):
    inv = ROPE_THETA ** (-jnp.arange(0, HALF, 2, dtype=F32) / HALF)
    ang = pos.astype(F32)[:, None] * inv[None, :]
    cos = jnp.concatenate([jnp.cos(ang)] * 4, axis=-1)
    sin = jnp.concatenate([-jnp.sin(ang), jnp.sin(ang)] * 2, axis=-1)
    return cos, sin


def _diff_finish(acc, l, lam, g, lam_init, rows):
    o = acc[:rows] / l[:rows] - lam * (acc[rows:] / l[rows:])
    o = o * lax.rsqrt(jnp.mean(o * o, axis=-1, keepdims=True) + RMS_EPS)
    return o * g * (1.0 - lam_init)


ONES_ROWS = 16


def _with_ones_rows(vt):
    return jnp.concatenate([vt, jnp.ones(vt.shape[:-2] + (ONES_ROWS, vt.shape[-1]), vt.dtype)], axis=-2)


def _diff_kernel(lam_ref, q_ref, k_ref, vt_ref, g_ref, o_ref, m_scr, acc_scr, *, tq, lam_init):
    qi = pl.program_id(2)
    q = q_ref[0]
    lane = _lane_iota(q.shape)
    zero = jnp.zeros_like(q)
    qq = jnp.concatenate([jnp.where(lane < HALF, q, zero), jnp.where(lane >= HALF, q, zero)], axis=0)
    m_scr[...] = jnp.full(m_scr.shape, NEG_BIG, F32)
    acc_scr[...] = jnp.zeros(acc_scr.shape, F32)

    def step(j, diagonal):
        off = pl.multiple_of(j * tq, tq)
        s = _nt_dot(k_ref[0, pl.ds(off, tq), :], qq)
        if diagonal:
            kpos = lax.broadcasted_iota(I32, s.shape, 0)
            qpos = lax.broadcasted_iota(I32, s.shape, 1) % tq
            s = jnp.where(kpos <= qpos, s, NEG_BIG)
        m_old = m_scr[...]
        m_new = jnp.maximum(m_old, jnp.max(s, axis=0, keepdims=True))
        a = jnp.exp2(m_old - m_new)
        p = jnp.exp2(s - m_new)
        acc_scr[...] = a * acc_scr[...] + jnp.dot(vt_ref[0, 0, j], p.astype(BF16), preferred_element_type=F32)
        m_scr[...] = m_new

    def body(j, carry):
        step(j, False)
        return carry

    lax.fori_loop(0, qi, body, 0)
    step(qi, True)
    o = acc_scr[:LANES, :] * (1.0 / acc_scr[LANES:LANES + 1, :])
    o = o[:, :tq] - lam_ref[0, 0] * o[:, tq:]
    o = o * lax.rsqrt(jnp.mean(o * o, axis=0, keepdims=True) + RMS_EPS)
    o = o * g_ref[...] * (1.0 - lam_init)
    o_ref[0] = o.T.astype(o_ref.dtype)


def _diff_attention(lam, qa, ka, va, g, lam_init, tq):
    b, t, _ = qa.shape
    nblk = t // tq
    vt = _with_ones_rows(jnp.transpose(va.reshape(b, nblk, tq, A_HEADS, LANES), (0, 3, 1, 4, 2)))
    rows = LANES + ONES_ROWS
    return pl.pallas_call(
        functools.partial(_diff_kernel, tq=tq, lam_init=lam_init),
        grid=(b, A_HEADS, nblk),
        in_specs=[pl.BlockSpec(memory_space=pltpu.SMEM),
                  pl.BlockSpec((1, tq, LANES), lambda b_, h, i: (b_, i, h)),
                  pl.BlockSpec((1, t, LANES), lambda b_, h, i: (b_, 0, h)),
                  pl.BlockSpec((1, 1, nblk, rows, tq), lambda b_, h, i: (b_, h, 0, 0, 0)),
                  pl.BlockSpec((LANES, 1), lambda b_, h, i: (0, 0))],
        out_specs=pl.BlockSpec((1, tq, LANES), lambda b_, h, i: (b_, i, h)),
        out_shape=jax.ShapeDtypeStruct((b, t, A_HEADS * LANES), BF16),
        scratch_shapes=[pltpu.VMEM((1, 2 * tq), F32), pltpu.VMEM((rows, 2 * tq), F32)],
        compiler_params=_cparams(("parallel", "parallel", "arbitrary")),
        name="diff_attn",
    )(lam, qa, ka, vt, g.reshape(LANES, 1))


def _float_key(x):
    b = lax.bitcast_convert_type(x + 0.0, I32)
    return b ^ ((b >> 31) & 0x7FFFFFFF)


I16 = jnp.int16
I16_MIN = -(2 ** 15)


def _split_key(key):
    return (key >> 16).astype(I16), ((key & 0xFFFF) - 2 ** 15).astype(I16)


def _select_topk(key_scr, hi_scr, lo_scr, nch, ck, topk):
    nq = key_scr.shape[1]
    kf = float(topk)
    grp = 64
    assert key_scr.shape[0] // grp < 2 ** 15

    def counts(plane, thr, strict):
        thr_b = jnp.broadcast_to(thr.astype(plane.dtype), (grp, nq))
        one, zero = jnp.ones((), plane.dtype), jnp.zeros((), plane.dtype)

        def body(c, cnt):
            off = pl.multiple_of(c * ck, ck)
            for g in range(ck // grp):
                kk = plane[pl.ds(off + g * grp, grp), :]
                hit = (kk > thr_b) if strict else (kk >= thr_b)
                cnt = cnt + jnp.where(hit, one, zero)
            return cnt
        cnt = lax.fori_loop(0, nch, body, jnp.zeros((grp, nq), plane.dtype))
        return jnp.sum(cnt.astype(F32), axis=0, keepdims=True)

    def search16(plane, base):
        tau = jnp.where(base + counts(plane, jnp.zeros((1, nq), I32), False) >= kf, 0, I16_MIN).astype(I32)

        def bit_body(i, tau):
            cand = tau | jnp.left_shift(jnp.int32(1), 14 - i)
            return jnp.where(base + counts(plane, cand, False) >= kf, cand, tau)

        return lax.fori_loop(0, 15, bit_body, tau)

    tau_hi = search16(hi_scr, jnp.zeros((1, nq), F32))
    above = counts(hi_scr, tau_hi, True)
    hi_b = jnp.broadcast_to(tau_hi.astype(I16), (grp, nq))

    def mask_body(c, carry):
        off = pl.multiple_of(c * ck, ck)
        for g in range(ck // grp):
            sl = pl.ds(off + g * grp, grp)
            lo_scr[sl, :] = jnp.where(hi_scr[sl, :] == hi_b, lo_scr[sl, :], jnp.full((), I16_MIN, I16))
        return carry

    lax.fori_loop(0, nch, mask_body, 0)
    tau_lo = search16(lo_scr, above)
    tau = tau_hi * 2 ** 16 + (tau_lo + 2 ** 15)

    def bcast(v, rows=grp):
        return jnp.broadcast_to(v, (rows, nq))

    cnt_ge = counts(key_scr, tau, False)
    cnt_gt = counts(key_scr, tau, True)
    overfull = jnp.where((cnt_ge > kf) & (tau > INT_MIN), 1.0, 0.0)

    @pl.when(jnp.max(overfull) > 0.0)
    def _():
        tau_b = bcast(tau, LANES)
        need_b = bcast(kf - cnt_gt, LANES)
        lower = jnp.where(lax.broadcasted_iota(I32, (LANES, LANES), 0)
                          >= lax.broadcasted_iota(I32, (LANES, LANES), 1), 1.0, 0.0).astype(BF16)

        def body(c, run):
            off = pl.multiple_of(c * ck, ck)
            for g in range(ck // LANES):
                sl = pl.ds(off + g * LANES, LANES)
                kk = key_scr[sl, :]
                eq = kk == tau_b
                eqf = jnp.where(eq, 1.0, 0.0)
                pre = jnp.dot(lower, eqf.astype(BF16), preferred_element_type=F32)
                drop = eq & ((pre + run) > need_b)
                key_scr[sl, :] = jnp.where(drop, INT_MIN, kk)
                run = run + jnp.sum(eqf, axis=0, keepdims=True)
            return run

        lax.fori_loop(0, nch, body, jnp.zeros((1, nq), F32))

    return tau


def _stack_heads(q, n_heads):
    lane = _lane_iota((q.shape[0], LANES))
    zero = jnp.zeros((q.shape[0], LANES), q.dtype)
    parts = []
    for h in range(n_heads):
        blk = q[:, (h // 2) * LANES:(h // 2 + 1) * LANES]
        parts.append(jnp.where((lane < HALF) == (h % 2 == 0), blk, zero))
    return jnp.concatenate(parts, axis=0)


M_INIT = -1e29


def _sparse_kernel(qi_ref, wt_ref, ki_ref, qb_ref, kb_ref, vt_ref, o_ref, key_scr, hi_scr, lo_scr, m_scr, acc_scr,
                   *, tq, ck, topk):
    i = pl.program_id(1)
    nch = ((i + 1) * tq + ck - 1) // ck
    kpos0 = lax.broadcasted_iota(I32, (ck, tq), 0)
    qpos = i * tq + lax.broadcasted_iota(I32, (ck, tq), 1)

    qs = _stack_heads(qi_ref[0], IDX_HEADS)
    wt = wt_ref[0]

    def score_body(c, carry):
        off = pl.multiple_of(c * ck, ck)
        d = _nt_dot(ki_ref[0, pl.ds(off, ck), :], qs)
        sc = jnp.zeros((ck, tq), F32)
        for h in range(IDX_HEADS):
            sc = sc + jnp.maximum(d[:, h * tq:(h + 1) * tq], 0.0) * wt[h:h + 1, :]
        sc = sc * IDX_SCALE
        key = jnp.where(kpos0 + off <= qpos, _float_key(sc), INT_MIN)
        key_scr[pl.ds(off, ck), :] = key
        hi_scr[pl.ds(off, ck), :], lo_scr[pl.ds(off, ck), :] = _split_key(key)
        return carry

    lax.fori_loop(0, nch, score_body, 0)
    tau = _select_topk(key_scr, hi_scr, lo_scr, nch, ck, topk)
    tau_b = jnp.broadcast_to(tau, (ck, tq))

    qh = []
    qb = qb_ref[0]
    lane = _lane_iota((tq, LANES))
    zero = jnp.zeros((tq, LANES), qb.dtype)
    grp = B_HEADS // B_KV_HEADS
    for h in range(B_HEADS):
        blk = qb[:, (h // 2) * LANES:(h // 2 + 1) * LANES]
        if h % 2 != h // grp:
            blk = pltpu.roll(blk, HALF, 1)
        qh.append(jnp.where((lane >= HALF) == (h // grp == 1), blk, zero))
    qs2 = jnp.concatenate(qh, axis=0)
    m_scr[...] = jnp.full(m_scr.shape, M_INIT, F32)
    acc_scr[...] = jnp.zeros(acc_scr.shape, F32)

    def attn_body(c, carry):
        off = pl.multiple_of(c * ck, ck)
        sel = (key_scr[pl.ds(off, ck), :] >= tau_b) & (kpos0 + off <= qpos)
        bias = jnp.where(sel, 0.0, NEG_BIG)
        s = _nt_dot(kb_ref[0, pl.ds(off, ck), :], qs2)
        ps, scales = [], []
        for h in range(B_HEADS):
            hs = slice(h * tq, (h + 1) * tq)
            sh = s[:, hs] + bias
            m_old = m_scr[:, hs]
            m_new = jnp.maximum(m_old, jnp.max(sh, axis=0, keepdims=True))
            a = jnp.exp2(m_old - m_new)
            p = jnp.exp2(sh - m_new)
            m_scr[:, hs] = m_new
            ps.append(p.astype(BF16))
            scales.append(a)
        pv = jnp.dot(vt_ref[0, c], jnp.concatenate(ps, axis=1), preferred_element_type=F32)
        acc_scr[...] = jnp.concatenate(scales, axis=1) * acc_scr[...] + pv
        return carry

    lax.fori_loop(0, nch, attn_body, 0)
    o = acc_scr[:LANES, :] * (1.0 / acc_scr[LANES:LANES + 1, :])
    parts = [o[(h // grp) * HALF:(h // grp + 1) * HALF, h * tq:(h + 1) * tq] for h in range(B_HEADS)]
    o_ref[0] = jnp.concatenate(parts, axis=0).T.astype(o_ref.dtype)


def _sparse_attention(qi, wi, ki2, qb, kb, vb, tq, ck):
    b, t, _ = qi.shape
    topk = min(TOPK_MAX, t // 4)
    wt = jnp.transpose(wi, (0, 2, 1))
    vt = _with_ones_rows(jnp.transpose(vb.reshape(b, t // ck, ck, LANES), (0, 1, 3, 2)))
    rows = LANES + ONES_ROWS
    blk_q = lambda w: pl.BlockSpec((1, tq, w), lambda b_, i: (b_, i, 0))
    blk_k = pl.BlockSpec((1, t, LANES), lambda b_, i: (b_, 0, 0))
    return pl.pallas_call(
        functools.partial(_sparse_kernel, tq=tq, ck=ck, topk=topk),
        grid=(b, t // tq),
        in_specs=[blk_q(512), pl.BlockSpec((1, IDX_HEADS, tq), lambda b_, i: (b_, 0, i)), blk_k, blk_q(512), blk_k,
                  pl.BlockSpec((1, t // ck, rows, ck), lambda b_, i: (b_, 0, 0, 0))],
        out_specs=blk_q(512),
        out_shape=jax.ShapeDtypeStruct((b, t, 512), BF16),
        scratch_shapes=[pltpu.VMEM((t, tq), I32), pltpu.VMEM((t, tq), I16), pltpu.VMEM((t, tq), I16),
                        pltpu.VMEM((1, B_HEADS * tq), F32),
                        pltpu.VMEM((rows, B_HEADS * tq), F32)],
        compiler_params=_cparams(("parallel", "arbitrary")),
        name="sparse_attn",
    )(qi, wt, ki2, qb, kb, vt)


def _page_specs(n, block, pg):
    zeros = (0,) * (len(block) - 1)
    return [pl.BlockSpec(block, functools.partial(lambda i, b, j, pt: (pt[b, j * pg + i],) + zeros, i))
            for i in range(n)]


def _dec_diff_kernel(pt_ref, lam_ref, q_ref, kn_ref, vn_ref, g_ref, qi_ref, w_ref, kin_ref, *rest, pg, lam_init):
    k_refs, v_refs, ki_refs = rest[:pg], rest[pg:2 * pg], rest[2 * pg:3 * pg]
    o_ref, sc_ref, new_ref, m_scr, l_scr, acc_scr, kbuf, vbuf, kibuf = rest[3 * pg:]
    j = pl.program_id(1)

    @pl.when(j == 0)
    def _():
        m_scr[...] = jnp.full(m_scr.shape, NEG_BIG, F32)
        l_scr[...] = jnp.zeros(l_scr.shape, F32)
        acc_scr[...] = jnp.zeros(acc_scr.shape, F32)

    rows_pg = k_refs[0].shape[1]
    page = ki_refs[0].shape[2]
    for i in range(pg):
        kbuf[i * rows_pg:(i + 1) * rows_pg, :] = k_refs[i][0].astype(BF16)
        vbuf[i * rows_pg:(i + 1) * rows_pg, :] = v_refs[i][0].astype(BF16)
        kibuf[:, i * page:(i + 1) * page] = ki_refs[i][0].astype(BF16)

    q = q_ref[0]
    s = _nt_dot(q, kbuf[...])
    same_head = (_lane_iota(s.shape) % A_HEADS) == (lax.broadcasted_iota(I32, s.shape, 0) % A_HEADS)
    s = jnp.where(same_head, s, NEG_BIG)
    m_old = m_scr[...]
    m_new = jnp.maximum(m_old, jnp.max(s, axis=1, keepdims=True))
    a = jnp.exp2(m_old - m_new)
    p = jnp.exp2(s - m_new).astype(BF16)
    l_scr[...] = a * l_scr[...] + jnp.sum(p.astype(F32), axis=1, keepdims=True)
    acc_scr[...] = a * acc_scr[...] + jnp.dot(p, vbuf[...], preferred_element_type=F32)
    m_scr[...] = m_new

    qi = qi_ref[0]
    w = w_ref[0]
    d = jnp.dot(qi, kibuf[...], preferred_element_type=F32)
    sc_ref[0] = jnp.sum(jnp.maximum(d, 0.0) * w, axis=0, keepdims=True) * IDX_SCALE

    @pl.when(j == pl.num_programs(1) - 1)
    def _():
        s_new = jnp.sum(q.astype(F32) * kn_ref[0], axis=1, keepdims=True)
        m_old = m_scr[...]
        m_new = jnp.maximum(m_old, s_new)
        a = jnp.exp2(m_old - m_new)
        pn = jnp.exp2(s_new - m_new)
        l = a * l_scr[...] + pn
        acc = a * acc_scr[...] + pn * vn_ref[0]
        o = _diff_finish(acc, l, lam_ref[0, 0], g_ref[...], lam_init, A_HEADS)
        o_ref[0] = jnp.concatenate([o, jnp.zeros_like(o)], axis=0)
        dn = jnp.sum(qi.astype(F32) * kin_ref[0], axis=1, keepdims=True)
        scn = jnp.sum(jnp.maximum(dn, 0.0) * w, axis=0, keepdims=True) * IDX_SCALE
        new_ref[0] = jnp.broadcast_to(scn, new_ref.shape[1:])


def _dec_diff_attention(pt, lam, qm, knew, vnew, g, qi, wi, kinew, kpages, vpages, kipages, lam_init, pg):
    db, npg = pt.shape
    rows_pg = kpages.shape[1]
    page = kipages.shape[2]
    per_b = lambda shape: pl.BlockSpec(shape, lambda b, j, pt_: (b,) + (0,) * (len(shape) - 1))
    return pl.pallas_call(
        functools.partial(_dec_diff_kernel, pg=pg, lam_init=lam_init),
        grid_spec=pltpu.PrefetchScalarGridSpec(
            num_scalar_prefetch=1,
            grid=(db, npg // pg),
            in_specs=[pl.BlockSpec(memory_space=pltpu.SMEM), per_b((1, 8, LANES)), per_b((1, 8, LANES)),
                      per_b((1, 8, LANES)), pl.BlockSpec((1, LANES), lambda b, j, pt_: (0, 0)),
                      per_b((1, IDX_HEADS, IDX_DIM)), per_b((1, IDX_HEADS, 1)), per_b((1, 1, IDX_DIM))]
                     + _page_specs(pg, (1,) + kpages.shape[1:], pg) + _page_specs(pg, (1,) + vpages.shape[1:], pg)
                     + _page_specs(pg, (1,) + kipages.shape[1:], pg),
            out_specs=[per_b((1, 8, LANES)), pl.BlockSpec((1, 1, pg * page), lambda b, j, pt_: (b, 0, j)),
                       per_b((1, 1, LANES))],
            scratch_shapes=[pltpu.VMEM((8, 1), F32), pltpu.VMEM((8, 1), F32), pltpu.VMEM((8, LANES), F32),
                            pltpu.VMEM((pg * rows_pg, LANES), BF16), pltpu.VMEM((pg * rows_pg, LANES), BF16),
                            pltpu.VMEM((IDX_DIM, pg * page), BF16)],
        ),
        out_shape=[jax.ShapeDtypeStruct((db, 8, LANES), F32), jax.ShapeDtypeStruct((db, 1, npg * page), F32),
                   jax.ShapeDtypeStruct((db, 1, LANES), F32)],
        compiler_params=_cparams(("parallel", "arbitrary")),
        name="dec_diff_attn",
    )(pt, lam, qm, knew, vnew, g, qi, wi, kinew, *([kpages] * pg), *([vpages] * pg), *([kipages] * pg))


def _dec_select_kernel(sc_ref, key_ref, tau_ref, hi_scr, lo_scr, *, n_keys, topk):
    width, nq = sc_ref.shape
    kpos = lax.broadcasted_iota(I32, (width, nq), 0)
    key = jnp.where(kpos < n_keys, _float_key(sc_ref[...]), INT_MIN)
    key_ref[...] = key
    hi_scr[...], lo_scr[...] = _split_key(key)
    tau = _select_topk(key_ref, hi_scr, lo_scr, width // LANES, LANES, topk)
    tau_ref[...] = jnp.broadcast_to(tau, tau_ref.shape)


def _dec_select(scores_t, n_keys, topk):
    width, nq = scores_t.shape
    return pl.pallas_call(
        functools.partial(_dec_select_kernel, n_keys=n_keys, topk=topk),
        out_shape=[jax.ShapeDtypeStruct((width, nq), I32), jax.ShapeDtypeStruct((8, nq), I32)],
        scratch_shapes=[pltpu.VMEM((width, nq), I16), pltpu.VMEM((width, nq), I16)],
        compiler_params=pltpu.CompilerParams(vmem_limit_bytes=VMEM_LIMIT),
        name="dec_select",
    )(scores_t)


def _dec_sparse_kernel(pt_ref, q_ref, key_ref, keyn_ref, tau_ref, kn_ref, vn_ref, *rest, pg):
    k_refs, v_refs = rest[:pg], rest[pg:2 * pg]
    o_ref, m_scr, l_scr, acc_scr, kbuf, vbuf = rest[2 * pg:]
    j = pl.program_id(1)

    @pl.when(j == 0)
    def _():
        m_scr[...] = jnp.full(m_scr.shape, NEG_BIG, F32)
        l_scr[...] = jnp.zeros(l_scr.shape, F32)
        acc_scr[...] = jnp.zeros(acc_scr.shape, F32)

    page = k_refs[0].shape[3]
    for i in range(pg):
        kbuf[:, :, i * page:(i + 1) * page] = k_refs[i][0].astype(BF16)
        vbuf[:, :, i * page:(i + 1) * page] = v_refs[i][0].astype(BF16)
    tau = tau_ref[0][:, 0:1]
    sel = jnp.broadcast_to(key_ref[0] >= tau, (8, pg * page))
    for g in range(B_KV_HEADS):
        q = q_ref[0, g]
        s = jnp.dot(q, kbuf[g], preferred_element_type=F32)
        s = jnp.where(sel, s, NEG_BIG)
        m_old = m_scr[g]
        m_new = jnp.maximum(m_old, jnp.max(s, axis=1, keepdims=True))
        a = jnp.exp2(m_old - m_new)
        p = jnp.where(sel, jnp.exp2(s - m_new), 0.0).astype(BF16)
        l_scr[g] = a * l_scr[g] + jnp.sum(p.astype(F32), axis=1, keepdims=True)
        acc_scr[g] = a * acc_scr[g] + _nt_dot(p, vbuf[g])
        m_scr[g] = m_new

    @pl.when(j == pl.num_programs(1) - 1)
    def _():
        sel_new = keyn_ref[0][:, 0:1] >= tau
        for g in range(B_KV_HEADS):
            s_new = jnp.sum(q_ref[0, g].astype(F32) * kn_ref[0, g], axis=1, keepdims=True)
            s_new = jnp.where(sel_new, s_new, NEG_BIG)
            m_old = m_scr[g]
            m_new = jnp.maximum(m_old, s_new)
            a = jnp.exp2(m_old - m_new)
            pn = jnp.where(sel_new, jnp.exp2(s_new - m_new), 0.0)
            l = a * l_scr[g] + pn
            o_ref[0, g] = (a * acc_scr[g] + pn * vn_ref[0, g]) / l


def _dec_sparse_attention(pt, q, keys, tau, kbnew, vbnew, kpages, vpages, pg):
    db, npg = pt.shape
    page = kpages.shape[3]
    per_b = lambda shape: pl.BlockSpec(shape, lambda b, j, pt_: (b,) + (0,) * (len(shape) - 1))
    blk = (1,) + kpages.shape[1:]
    return pl.pallas_call(
        functools.partial(_dec_sparse_kernel, pg=pg),
        grid_spec=pltpu.PrefetchScalarGridSpec(
            num_scalar_prefetch=1,
            grid=(db, npg // pg),
            in_specs=[per_b((1, B_KV_HEADS, 8, B_HEAD_DIM)),
                      pl.BlockSpec((1, 1, pg * page), lambda b, j, pt_: (b, 0, j)),
                      pl.BlockSpec((1, 1, LANES), lambda b, j, pt_: (b, 0, npg * page // LANES)),
                      per_b((1, 1, LANES)), per_b((1, B_KV_HEADS, 1, B_HEAD_DIM)),
                      per_b((1, B_KV_HEADS, 1, B_HEAD_DIM))]
                     + _page_specs(pg, blk, pg) + _page_specs(pg, blk, pg),
            out_specs=per_b((1, B_KV_HEADS, 8, B_HEAD_DIM)),
            scratch_shapes=[pltpu.VMEM((B_KV_HEADS, 8, 1), F32), pltpu.VMEM((B_KV_HEADS, 8, 1), F32),
                            pltpu.VMEM((B_KV_HEADS, 8, B_HEAD_DIM), F32),
                            pltpu.VMEM((B_KV_HEADS, B_HEAD_DIM, pg * page), BF16),
                            pltpu.VMEM((B_KV_HEADS, B_HEAD_DIM, pg * page), BF16)],
        ),
        out_shape=jax.ShapeDtypeStruct((db, B_KV_HEADS, 8, B_HEAD_DIM), F32),
        compiler_params=_cparams(("parallel", "arbitrary")),
        name="dec_sparse_attn",
    )(pt, q, keys, keys, tau, kbnew, vbnew, *([kpages] * pg), *([vpages] * pg))


def _layer_norm(x, g, b):
    mu = jnp.mean(x, axis=-1, keepdims=True)
    xc = x - mu
    var = jnp.mean(xc * xc, axis=-1, keepdims=True)
    return xc * lax.rsqrt(var + LN_EPS) * g + b


def _post_kernel(x_ref, ma_ref, mb_ref, woa_ref, wob_ref, g_ref, b_ref, rw_ref, rb_ref, h_ref, c_ref):
    y = (DEEPNORM_ALPHA * x_ref[...]
         + jnp.dot(ma_ref[...], woa_ref[...], preferred_element_type=F32)
         + jnp.dot(mb_ref[...], wob_ref[...], preferred_element_type=F32))
    h = _layer_norm(y, g_ref[...], b_ref[...])
    h_ref[...] = h
    logits = jnp.dot(h, rw_ref[...], preferred_element_type=F32, precision=lax.Precision.HIGHEST) + rb_ref[...]
    lane = _lane_iota(logits.shape)
    work = logits
    sel = jnp.zeros(logits.shape, jnp.bool_)
    for _ in range(TOP_K):
        mx = jnp.max(work, axis=-1, keepdims=True)
        idx = jnp.min(jnp.where(work == mx, lane, N_EXPERTS), axis=-1, keepdims=True)
        pick = lane == idx
        sel = sel | pick
        work = jnp.where(pick, -jnp.inf, work)
    e = jnp.where(sel, jnp.exp(logits - jnp.max(logits, axis=-1, keepdims=True)), 0.0)
    c_ref[...] = e / jnp.sum(e, axis=-1, keepdims=True)


def _post_mixer(x, mix_a, mix_b, wo_a, wo_b, g, b, rw, rb, tm):
    n, d = x.shape
    row = lambda w: pl.BlockSpec((tm, w), lambda i: (i, 0))
    full = lambda a: pl.BlockSpec(a.shape, lambda i: (0, 0))
    return pl.pallas_call(
        _post_kernel,
        grid=(n // tm,),
        in_specs=[row(d), row(mix_a.shape[1]), row(mix_b.shape[1]), full(wo_a), full(wo_b), full(g), full(b),
                  full(rw), full(rb)],
        out_specs=[row(d), row(N_EXPERTS)],
        out_shape=[jax.ShapeDtypeStruct((n, d), F32), jax.ShapeDtypeStruct((n, N_EXPERTS), F32)],
        compiler_params=_cparams(("parallel",)),
        name="post_mixer",
    )(x, mix_a, mix_b, wo_a, wo_b, g, b, rw, rb)


W_SPLIT = 4


def _moe_kernel(cnt_ref, h_ref, comb_ref, upper_ref, bg_ref, bu_ref, bd_ref, g_ref, b_ref, *rest, tt, cs, nsub):
    wg_refs, wu_refs, wd_refs = rest[:W_SPLIT], rest[W_SPLIT:2 * W_SPLIT], rest[2 * W_SPLIT:3 * W_SPLIT]
    o_ref, xs_scr, pm_scr, gs_scr, ys_scr = rest[3 * W_SPLIT:]
    _moe_body(cnt_ref, h_ref, comb_ref, upper_ref, wg_refs, bg_ref, wu_refs, bu_ref, wd_refs, bd_ref,
              g_ref, b_ref, o_ref, xs_scr, pm_scr, gs_scr, ys_scr, tt=tt, cs=cs, nsub=nsub)


def _chunked_dot(x, w_refs):
    kc = w_refs[0].shape[1]
    acc = jnp.dot(x[:, :kc], w_refs[0][0], preferred_element_type=F32)
    for r in range(1, len(w_refs)):
        acc = acc + jnp.dot(x[:, r * kc:(r + 1) * kc], w_refs[r][0], preferred_element_type=F32)
    return acc


def _moe_body(cnt_ref, h_ref, comb_ref, upper_ref, wg_refs, bg_ref, wu_refs, bu_ref, wd_refs, bd_ref,
              g_ref, b_ref, o_ref, xs_scr, pm_scr, gs_scr, ys_scr, *, tt, cs, nsub):
    i = pl.program_id(0)
    e = pl.program_id(1)
    n_exp = pl.num_programs(1)

    @pl.when(e == 0)
    def _():
        o_ref[...] = jnp.zeros(o_ref.shape, F32)

    slot0 = lax.broadcasted_iota(I32, (cs, tt), 0).astype(F32) + 1.0

    def onehot_of(j, first):
        gate = comb_ref[0, 0, pl.ds(j, 1), :]
        sel = gate > 0.0
        self_ = jnp.broadcast_to(jnp.where(sel, 1.0, 0.0), (8, tt)).astype(BF16)
        rank = jnp.dot(self_, upper_ref[...], preferred_element_type=F32)[0:1]
        onehot = jnp.broadcast_to(sel, (cs, tt)) & (jnp.broadcast_to(rank, (cs, tt)) == slot0 + first)
        return onehot, jnp.broadcast_to(gate, (cs, tt))

    def expert(xs):
        gg = _chunked_dot(xs, wg_refs) + bg_ref[0]
        uu = _chunked_dot(xs, wu_refs) + bu_ref[0]
        gg = jnp.minimum(gg, SWIGLU_LIMIT)
        uu = jnp.clip(uu, -SWIGLU_LIMIT, SWIGLU_LIMIT)
        hh = gg * jax.nn.sigmoid(SWIGLU_ALPHA * gg) * (uu + 1.0)
        return _chunked_dot(hh.astype(BF16), wd_refs) + bd_ref[0]

    odd = e % 2
    half = pl.ds(pl.multiple_of(odd * cs, 16), cs)
    for j in range(nsub):
        onehot, gate_b = onehot_of(j, 0.0)
        pm = jnp.where(onehot, 1.0, 0.0).astype(BF16)
        pm_scr[j, half, :] = pm
        xs_scr[j * cs:(j + 1) * cs, :] = jnp.dot(pm, h_ref[j * tt:(j + 1) * tt, :].astype(BF16),
                                                 preferred_element_type=F32).astype(BF16)
        gs_scr[j * cs:(j + 1) * cs, :] = jnp.sum(jnp.where(onehot, gate_b, 0.0), axis=1, keepdims=True)
    ys = (expert(xs_scr[...]) * gs_scr[...]).astype(BF16)
    for j in range(nsub):
        ys_scr[j, half, :] = ys[j * cs:(j + 1) * cs]

    @pl.when(odd == 1)
    def _():
        for j in range(nsub):
            o_ref[j * tt:(j + 1) * tt, :] += _tn_dot(pm_scr[j], ys_scr[j])

    def sub_body(j, carry):
        n = cnt_ref[(e * pl.num_programs(0) + i) * nsub + j]

        @pl.when(n > cs)
        def _():
            off = pl.multiple_of(j * tt, tt)
            xj = h_ref[pl.ds(off, tt), :].astype(BF16)

            def chunk_body(c, carry2):
                onehot, gate_b = onehot_of(j, (c * cs).astype(F32))
                pm = jnp.where(onehot, 1.0, 0.0).astype(BF16)
                y = expert(jnp.dot(pm, xj, preferred_element_type=F32).astype(BF16))
                gs = jnp.sum(jnp.where(onehot, gate_b, 0.0), axis=1, keepdims=True)
                o_ref[pl.ds(off, tt), :] += _tn_dot(pm, (y * gs).astype(BF16))
                return carry2

            lax.fori_loop(1, (n + cs - 1) // cs, chunk_body, 0)

        return carry

    lax.fori_loop(0, nsub, sub_body, 0)

    @pl.when(e == n_exp - 1)
    def _():
        o_ref[...] = _layer_norm(DEEPNORM_ALPHA * h_ref[...] + o_ref[...], g_ref[...], b_ref[...])


def _moe(h, comb, wg, bg, wu, bu, wd, bd, g, b, tb, tt, cs):
    n, d = h.shape
    n_exp, _, f = wg.shape
    assert n_exp % 2 == 0
    nsub = tb // tt
    nblk = n // tb
    comb_t = comb.T
    counts = jnp.sum((comb_t > 0.0).reshape(n_exp * nblk * nsub, tt), axis=-1).astype(I32)
    comb_t = comb_t.reshape(n_exp, nblk, nsub, tt)
    upper = (jnp.arange(tt)[:, None] <= jnp.arange(tt)[None, :]).astype(BF16)
    wspec = lambda a, b_: pl.BlockSpec((1, a, b_), lambda i, e, c: (e, 0, 0))
    wchunks = lambda a, b_: [pl.BlockSpec((1, a // W_SPLIT, b_), functools.partial(lambda r, i, e, c: (e, r, 0), r))
                             for r in range(W_SPLIT)]
    vec = pl.BlockSpec((1, d), lambda i, e, c: (0, 0))
    return pl.pallas_call(
        functools.partial(_moe_kernel, tt=tt, cs=cs, nsub=nsub),
        grid_spec=pltpu.PrefetchScalarGridSpec(
            num_scalar_prefetch=1,
            grid=(nblk, n_exp),
            in_specs=[pl.BlockSpec((tb, d), lambda i, e, c: (i, 0), pipeline_mode=pl.Buffered(1)),
                      pl.BlockSpec((1, 1, nsub, tt), lambda i, e, c: (e, i, 0, 0)),
                      pl.BlockSpec((tt, tt), lambda i, e, c: (0, 0)),
                      wspec(1, f), wspec(1, f), wspec(1, d), vec, vec]
                     + wchunks(d, f) + wchunks(d, f) + wchunks(f, d),
            out_specs=pl.BlockSpec((tb, d), lambda i, e, c: (i, 0)),
            scratch_shapes=[pltpu.VMEM((nsub * cs, d), BF16), pltpu.VMEM((nsub, 2 * cs, tt), BF16),
                            pltpu.VMEM((nsub * cs, 1), F32), pltpu.VMEM((nsub, 2 * cs, d), BF16)],
        ),
        out_shape=jax.ShapeDtypeStruct((n, d), F32),
        compiler_params=_cparams(("parallel", "arbitrary")),
        name="moe",
    )(counts, h, comb_t, upper, bg.reshape(n_exp, 1, f), bu.reshape(n_exp, 1, f), bd.reshape(n_exp, 1, d), g, b,
      *([wg] * W_SPLIT), *([wu] * W_SPLIT), *([wd] * W_SPLIT))


def _pad_w_in(w_in):
    d = w_in.shape[0]
    return jnp.concatenate([w_in, jnp.zeros((d, _IN_COLS_PAD - _IN_COLS), w_in.dtype)], axis=1).astype(BF16)


def _prep_params(p, l=0):
    half = p["w_out"].shape[1] // 2
    vec = lambda a: a[l].reshape(1, -1)
    return {
        "wo_a": p["w_out"][l, :half].astype(BF16), "wo_b": p["w_out"][l, half:].astype(BF16),
        "ln1_g": vec(p["ln1_g"]), "ln1_b": vec(p["ln1_b"]), "ln2_g": vec(p["ln2_g"]), "ln2_b": vec(p["ln2_b"]),
        "router_w": p["router_w"][l], "router_b": vec(p["router_b"]),
        "w_gate": p["w_gate"][l].astype(BF16), "b_gate": p["b_gate"][l],
        "w_up": p["w_up"][l].astype(BF16), "b_up": p["b_up"][l],
        "w_down": p["w_down"][l].astype(BF16), "b_down": p["b_down"][l],
    }


def _pick_tile(n, pref):
    t = min(n, pref)
    while n % t:
        t //= 2
    return t


def _token_major(a):
    a = jnp.moveaxis(a, -1, 1)
    return a.reshape((a.shape[0] * a.shape[1],) + a.shape[2:])


def _cache_rows(kaf, vaf, kbf, vbf, kif, b, t):
    return (kaf.reshape(1, b, t, A_HEADS, 2 * A_HEAD_DIM), vaf.reshape(1, b, t, A_HEADS, 2 * A_HEAD_DIM),
            _token_major(kbf).reshape(1, b, t, B_KV_HEADS, B_HEAD_DIM),
            _token_major(vbf).reshape(1, b, t, B_KV_HEADS, B_HEAD_DIM),
            _token_major(kif).reshape(1, b, t, IDX_DIM))


def _prompt_mixers(x_prompt, w_pad, lam, subln_g, lam_init):
    b, t, d = x_prompt.shape
    n = b * t
    cos, sin = _rope_tables(jnp.arange(t))
    (qa, kaf, kab, vaf, vab, qb, kbf, kbb, vbf, vbb, qi, kif, kib, wi) = _project(
        x_prompt.reshape(n, d), w_pad, cos, sin, _pick_tile(t, 512))
    r3 = lambda a: a.reshape(b, t, a.shape[-1])
    mix_a = _diff_attention(lam, r3(qa), r3(kab), r3(vab), subln_g, lam_init, _pick_tile(t, 1024))
    mix_b = _sparse_attention(r3(qi), r3(wi), r3(kib), r3(qb), r3(kbb), r3(vbb), _pick_tile(t, 512),
                              _pick_tile(t, 512))
    return mix_a.reshape(n, -1), mix_b.reshape(n, -1), _cache_rows(kaf, vaf, kbf, vbf, kif, b, t)


def _decode_mixers(x_sample, w_pad, lam, subln_g, lam_init, caches, page_table, pg):
    ck_diff, cv_diff, ck_sparse, cv_sparse, ck_index = caches
    db, _, d = x_sample.shape
    n_pool, page = ck_diff.shape[:2]
    npg = page_table.shape[1]
    past = npg * page
    cos, sin = _rope_tables(jnp.full((db,), past))
    (qa, kaf, _, vaf, _, qb, kbf, _, vbf, _, qi, kif, _, wi) = _project(x_sample.reshape(db, d), w_pad, cos, sin, db)

    qa3 = qa.reshape(db, A_HEADS, LANES)
    lane = _lane_iota(qa3.shape[1:])[None]
    zero = jnp.zeros_like(qa3)
    qm = jnp.concatenate([jnp.where(lane < HALF, qa3, zero), jnp.where(lane >= HALF, qa3, zero)], axis=1)
    twice = lambda a: jnp.concatenate([a.reshape(db, A_HEADS, LANES)] * 2, axis=1)
    out_a, scores, sc_new = _dec_diff_attention(
        page_table, lam, qm, twice(kaf), twice(vaf), subln_g,
        qi.reshape(db, IDX_HEADS, IDX_DIM), wi.reshape(db, IDX_HEADS, 1), _token_major(kif).reshape(db, 1, IDX_DIM),
        ck_diff.reshape(n_pool, page * A_HEADS, LANES), cv_diff.reshape(n_pool, page * A_HEADS, LANES),
        jnp.transpose(ck_index, (0, 2, 1)), lam_init, pg)
    mix_a = out_a[:, :A_HEADS].reshape(db, A_HEADS * LANES).astype(BF16)
    sc_all = jnp.concatenate([scores[:, 0], sc_new[:, 0, :1], jnp.zeros((db, LANES - 1), F32)], axis=1)
    keys_t, tau = _dec_select(sc_all.T, past + 1, min(TOPK_MAX, (past + 1) // 4))
    keys = keys_t.T
    tau = jnp.broadcast_to(tau[0].reshape(db, 1, 1), (db, 1, LANES))

    grp = B_HEADS // B_KV_HEADS
    qb4 = qb.reshape(db, B_KV_HEADS, grp, B_HEAD_DIM)
    qs = jnp.concatenate([qb4, jnp.zeros((db, B_KV_HEADS, 8 - grp, B_HEAD_DIM), qb.dtype)], axis=2)
    out_b = _dec_sparse_attention(page_table, qs, keys.reshape(db, 1, -1), tau,
                                  _token_major(kbf).reshape(db, B_KV_HEADS, 1, B_HEAD_DIM),
                                  _token_major(vbf).reshape(db, B_KV_HEADS, 1, B_HEAD_DIM),
                                  jnp.transpose(ck_sparse, (0, 2, 3, 1)), jnp.transpose(cv_sparse, (0, 2, 3, 1)),
                                  _pick_tile(npg, 2 * pg))
    mix_b = out_b[:, :, :grp].reshape(db, B_HEADS * B_HEAD_DIM).astype(BF16)
    return mix_a, mix_b, _cache_rows(kaf, vaf, kbf, vbf, kif, db, 1)


def _slot_rows(tt):
    pack = 16
    want = tt * TOP_K // N_EXPERTS
    want += want // 4
    return max(pack, -(-want // pack) * pack)


def _post_layer(x, mix_a, mix_b, p, tb, tt):
    n, d = x.shape
    h, comb = _post_mixer(x, mix_a, mix_b, p["wo_a"], p["wo_b"], p["ln1_g"], p["ln1_b"], p["router_w"],
                          p["router_b"], _pick_tile(n, 512))
    return _moe(h, comb, p["w_gate"], p["b_gate"], p["w_up"], p["b_up"], p["w_down"], p["b_down"],
                p["ln2_g"], p["ln2_b"], tb, tt, _slot_rows(tt))


def kernel(x_prompt, x_sample, cache_k_diff, cache_v_diff, cache_k_sparse, cache_v_sparse, cache_k_index, page_table,
           w_in, lambda_q1, lambda_k1, lambda_q2, lambda_k2, subln_g, w_out, ln1_g, ln1_b, router_w, router_b,
           w_gate, b_gate, w_up, b_up, w_down, b_down, ln2_g, ln2_b):
    assert w_in.shape[0] == DEPTH
    l = 0
    b, t, d = x_prompt.shape
    db = x_sample.shape[0]
    lam_init = 0.8 - 0.6 * math.exp(-0.3 * l)
    lam = (jnp.exp(jnp.sum(lambda_q1[l] * lambda_k1[l])) - jnp.exp(jnp.sum(lambda_q2[l] * lambda_k2[l]))
           + lam_init).reshape(1, 1)
    g = subln_g[l].reshape(1, -1)
    w_pad = _pad_w_in(w_in[l])
    params = _prep_params(dict(w_out=w_out, ln1_g=ln1_g, ln1_b=ln1_b, router_w=router_w, router_b=router_b,
                               w_gate=w_gate, b_gate=b_gate, w_up=w_up, b_up=b_up, w_down=w_down, b_down=b_down,
                               ln2_g=ln2_g, ln2_b=ln2_b), l)

    mix_a, mix_b, rows_p = _prompt_mixers(x_prompt, w_pad, lam, g, lam_init)
    n = b * t
    y_p = _post_layer(x_prompt.reshape(n, d), mix_a, mix_b, params, _pick_tile(n, 2048), _pick_tile(n, 512))

    caches = (cache_k_diff[l], cache_v_diff[l], cache_k_sparse[l], cache_v_sparse[l], cache_k_index[l])
    mix_a, mix_b, rows_s = _decode_mixers(x_sample, w_pad, lam, g, lam_init, caches, page_table,
                                          _pick_tile(page_table.shape[1], 16))
    y_s = _post_layer(x_sample.reshape(db, d), mix_a, mix_b, params, db, db)
    return (y_p.reshape(b, t, d), y_s.reshape(db, 1, d)) + rows_p + rows_s
```
